```python
import jax, jax.numpy as jnp
from jax import lax
import numpy as np

D_MODEL = 2048
BATCH = 16
SEQ = 2048
DEPTH = 2
DEC_BATCH = 2
DEC_SEQ = 4096
PAST_LEN = 128

GRID_W = 64
N_BRANCH = 4
BR_W = 512
HEAD_DIM = 64
CHUNK = 128
GM_GROUPS = 4
GM_GW = BR_W // GM_GROUPS
SW_HEADS = 8
SW_KV = 2
SW_WIN = 128
SW_BLOCK = 128
ROPE_THETA = 10000.0
CONV_W = 31
NA_HEADS = 8
NA_ROWS = 8
NA_COLS = 16
NA_BLOCK_W = 16
NA_SPAN_W = 2 * NA_COLS
FFN_HIDDEN = 5632
FFN_CONV = 3
EPS = 1e-6
NEG_INF = -1e30

IN_SIZES = (BR_W, BR_W, SW_HEADS * HEAD_DIM, SW_KV * HEAD_DIM, SW_KV * HEAD_DIM,
            BR_W, BR_W, NA_HEADS * HEAD_DIM, NA_HEADS * HEAD_DIM, NA_HEADS * HEAD_DIM,
            N_BRANCH * D_MODEL)
N_IN = sum(IN_SIZES)

kernel_name = "hybrid_gated_parallel_encoder"


def _rmsnorm(x, g):
    xf = x.astype(jnp.float32)
    y = xf * lax.rsqrt(jnp.mean(xf * xf, axis=-1, keepdims=True) + EPS)
    return (y * g.astype(jnp.float32)).astype(x.dtype)


def _layernorm(x, g, b):
    xf = x.astype(jnp.float32)
    mu = jnp.mean(xf, axis=-1, keepdims=True)
    xc = xf - mu
    y = xc * lax.rsqrt(jnp.mean(xc * xc, axis=-1, keepdims=True) + EPS)
    return (y * g.astype(jnp.float32) + b.astype(jnp.float32)).astype(x.dtype)


def _depthwise_conv(x, w, b):
    k = w.shape[0]
    y = lax.conv_general_dilated(x, w[:, None, :].astype(x.dtype), window_strides=(1,),
                                 padding=[(k // 2, k // 2)],
                                 dimension_numbers=('NWC', 'WIO', 'NWC'),
                                 feature_group_count=x.shape[-1])
    return y + b.astype(x.dtype)


def _rope(x, pos):
    half = x.shape[-1] // 2
    inv = jnp.power(ROPE_THETA, -jnp.arange(half, dtype=jnp.float32) / half)
    ang = pos.astype(jnp.float32)[:, None] * inv[None, :]
    cos = jnp.cos(ang)[:, None, :]
    sin = jnp.sin(ang)[:, None, :]
    xf = x.astype(jnp.float32)
    x1, x2 = xf[..., :half], xf[..., half:]
    return jnp.concatenate([x1 * cos - x2 * sin, x2 * cos + x1 * sin], axis=-1).astype(x.dtype)


def _spatial_gating(u, v, ln_g, ln_b, ws, bs):
    B, T, C = v.shape
    vn = _layernorm(v, ln_g, ln_b).reshape(B, T // CHUNK, CHUNK, GM_GROUPS, GM_GW)
    mixed = jnp.einsum('gpq,bnqgc->bnpgc', ws.astype(vn.dtype), vn) + bs.T.astype(vn.dtype)[None, None, :, :, None]
    return u * mixed.reshape(B, T, C)


def _sliding_window_attention(q, k, v, sink):
    B, T, H, dh = q.shape
    nb = T // SW_BLOCK
    g = H // SW_KV
    qb = q.reshape(B, nb, SW_BLOCK, SW_KV, g, dh)
    pad = ((0, 0), (SW_BLOCK, SW_BLOCK), (0, 0), (0, 0))

    def band(a):
        ab = jnp.pad(a, pad).reshape(B, nb + 2, SW_BLOCK, SW_KV, dh)
        return jnp.concatenate([ab[:, :-2], ab[:, 1:-1], ab[:, 2:]], axis=2)

    kb, vb = band(k), band(v)
    start = jnp.arange(nb)[:, None] * SW_BLOCK
    qpos = start + jnp.arange(SW_BLOCK)[None, :]
    kpos = start - SW_BLOCK + jnp.arange(3 * SW_BLOCK)[None, :]
    valid = ((jnp.abs(qpos[:, :, None] - kpos[:, None, :]) <= SW_WIN)
             & (kpos[:, None, :] >= 0) & (kpos[:, None, :] < T))
    s = jnp.einsum('bnqhgd,bnkhd->bnhgqk', qb, kb, preferred_element_type=jnp.float32) * (dh ** -0.5)
    s = jnp.where(valid[None, :, None, None], s, NEG_INF)
    sk = sink.astype(jnp.float32).reshape(SW_KV, g)[:, :, None]
    m = jnp.maximum(jnp.max(s, axis=-1), sk)
    p = jnp.exp(s - m[..., None])
    denom = jnp.sum(p, axis=-1) + jnp.exp(sk - m)
    w = (p / denom[..., None]).astype(v.dtype)
    out = jnp.einsum('bnhgqk,bnkhd->bnqhgd', w, vb)
    return out.reshape(B, T, H * dh)


def _conv_module(a, gate, dw, dwb, ln_g, ln_b):
    x = a * jax.nn.sigmoid(gate)
    x = _depthwise_conv(x, dw, dwb)
    x = _layernorm(x, ln_g, ln_b)
    return jax.nn.silu(x)


def _neighbourhood_attention(q, k, v, rpb):
    B, T, H, dh = q.shape
    rows = T // GRID_W
    wr = min(NA_ROWS, rows)
    ncb = GRID_W // NA_BLOCK_W
    qc = np.arange(GRID_W).reshape(ncb, NA_BLOCK_W)
    cs = np.clip(qc - NA_COLS // 2, 0, GRID_W - NA_COLS)
    ks = np.clip(np.arange(ncb) * NA_BLOCK_W - NA_COLS // 2, 0, GRID_W - NA_SPAN_W)
    kc = ks[:, None] + np.arange(NA_SPAN_W)[None, :]
    col_ok = (kc[:, None, :] >= cs[..., None]) & (kc[:, None, :] < cs[..., None] + NA_COLS)
    dc_idx = np.clip(kc[:, None, :] - qc[..., None], 1 - NA_COLS, NA_COLS - 1) + NA_COLS - 1
    qg = q.reshape(B, rows, GRID_W, H, dh)
    kg = k.reshape(B, rows, GRID_W, H, dh)
    vg = v.reshape(B, rows, GRID_W, H, dh)
    rpb_f = rpb.astype(jnp.float32)
    scale = dh ** -0.5

    def one_row(r):
        rs = jnp.clip(r - wr // 2, 0, rows - wr)
        k_blk = lax.dynamic_slice_in_dim(kg, rs, wr, axis=1)[:, :, kc]
        v_blk = lax.dynamic_slice_in_dim(vg, rs, wr, axis=1)[:, :, kc]
        q_blk = lax.dynamic_index_in_dim(qg, r, axis=1, keepdims=False).reshape(B, ncb, NA_BLOCK_W, H, dh)
        s = jnp.einsum('bcqhd,bwckhd->bhcqwk', q_blk, k_blk, preferred_element_type=jnp.float32) * scale
        dr_idx = rs + jnp.arange(wr) - r + NA_ROWS - 1
        bias = rpb_f[:, dr_idx][:, :, dc_idx].transpose(0, 2, 3, 1, 4)
        s = jnp.where(col_ok[:, :, None, :], s + bias, NEG_INF)
        p = jax.nn.softmax(s.reshape(B, H, ncb, NA_BLOCK_W, wr * NA_SPAN_W), axis=-1).reshape(s.shape)
        o = jnp.einsum('bhcqwk,bwckhd->bcqhd', p.astype(v.dtype), v_blk)
        return o.reshape(B, GRID_W, H, dh)

    out = lax.map(one_row, jnp.arange(rows))
    return out.transpose(1, 0, 2, 3, 4).reshape(B, T, H * dh)


def _conv_ffn(h, w_up, dw, dwb, w_down):
    z = _depthwise_conv(h @ w_up, dw, dwb)
    a, b = jnp.split(z, 2, axis=-1)
    return (jax.nn.silu(a) * b) @ w_down


def _trunk(x, norm1_g, w_in, gate_b, gm_ln_g, gm_ln_b, gm_ws, gm_bs, sw_sink,
           cv_dw, cv_dwb, cv_ln_g, cv_ln_b, na_rpb, w_branch, w_out,
           norm2_g, w_up, ffn_dw, ffn_dwb, w_down, final_g):
    B, T, _ = x.shape
    pos = jnp.arange(T)
    splits = np.cumsum(IN_SIZES)[:-1].tolist()
    for l in range(DEPTH):
        h = _rmsnorm(x, norm1_g[l])
        z = h @ w_in[l]
        (gm_u, gm_v, sw_q, sw_k, sw_v, cv_a, cv_g,
         na_q, na_k, na_v, gates) = jnp.split(z, splits, axis=-1)
        o_a = _spatial_gating(gm_u, gm_v, gm_ln_g[l], gm_ln_b[l], gm_ws[l], gm_bs[l])
        q = _rope(sw_q.reshape(B, T, SW_HEADS, HEAD_DIM), pos)
        k = _rope(sw_k.reshape(B, T, SW_KV, HEAD_DIM), pos)
        o_b = _sliding_window_attention(q, k, sw_v.reshape(B, T, SW_KV, HEAD_DIM), sw_sink[l])
        o_c = _conv_module(cv_a, cv_g, cv_dw[l], cv_dwb[l], cv_ln_g[l], cv_ln_b[l])
        o_d = _neighbourhood_attention(na_q.reshape(B, T, NA_HEADS, HEAD_DIM),
                                       na_k.reshape(B, T, NA_HEADS, HEAD_DIM),
                                       na_v.reshape(B, T, NA_HEADS, HEAD_DIM), na_rpb[l])
        gs = jax.nn.sigmoid(gates.reshape(B, T, N_BRANCH, D_MODEL) + gate_b[l])
        merged = (gs[:, :, 0] * (o_a @ w_branch[l, 0]) + gs[:, :, 1] * (o_b @ w_branch[l, 1])
                  + gs[:, :, 2] * (o_c @ w_branch[l, 2]) + gs[:, :, 3] * (o_d @ w_branch[l, 3]))
        x = x + merged @ w_out[l]
        x = x + _conv_ffn(_rmsnorm(x, norm2_g[l]), w_up[l], ffn_dw[l], ffn_dwb[l], w_down[l])
    return _rmsnorm(x, final_g)


def setup_inputs(seed: int = 0) -> dict:
    key = jax.random.key(seed)
    ks = jax.random.split(key, 26)
    L, D = DEPTH, D_MODEL

    def nrm(k, shape, scale):
        return jax.random.normal(k, shape, jnp.float32) * scale

    return {
        "x_prompt": nrm(ks[0], (BATCH, SEQ, D), 1.0),
        "x_sample": nrm(ks[1], (DEC_BATCH, DEC_SEQ, D), 1.0),
        "norm1_g": 1.0 + nrm(ks[2], (L, D), 0.02),
        "w_in": nrm(ks[3], (L, D, N_IN), D ** -0.5),
        "gate_b": nrm(ks[4], (L, N_BRANCH, D), 0.1),
        "gm_ln_g": 1.0 + nrm(ks[5], (L, BR_W), 0.02),
        "gm_ln_b": nrm(ks[6], (L, BR_W), 0.02),
        "gm_ws": nrm(ks[7], (L, GM_GROUPS, CHUNK, CHUNK), CHUNK ** -0.5),
        "gm_bs": 1.0 + nrm(ks[8], (L, GM_GROUPS, CHUNK), 0.02),
        "sw_sink": nrm(ks[9], (L, SW_HEADS), 0.5),
        "cv_dw": nrm(ks[10], (L, CONV_W, BR_W), CONV_W ** -0.5),
        "cv_dwb": nrm(ks[11], (L, BR_W), 0.02),
        "cv_ln_g": 1.0 + nrm(ks[12], (L, BR_W), 0.02),
        "cv_ln_b": nrm(ks[13], (L, BR_W), 0.02),
        "na_rpb": nrm(ks[14], (L, NA_HEADS, 2 * NA_ROWS - 1, 2 * NA_COLS - 1), 0.1),
        "w_branch": nrm(ks[15], (L, N_BRANCH, BR_W, D), BR_W ** -0.5),
        "w_out": nrm(ks[16], (L, D, D), D ** -0.5),
        "norm2_g": 1.0 + nrm(ks[17], (L, D), 0.02),
        "w_up": nrm(ks[18], (L, D, 2 * FFN_HIDDEN), D ** -0.5),
        "ffn_dw": nrm(ks[19], (L, FFN_CONV, 2 * FFN_HIDDEN), FFN_CONV ** -0.5),
        "ffn_dwb": nrm(ks[20], (L, 2 * FFN_HIDDEN), 0.02),
        "w_down": nrm(ks[21], (L, FFN_HIDDEN, D), FFN_HIDDEN ** -0.5),
        "final_g": 1.0 + nrm(ks[22], (D,), 0.02),
    }


def reference(x_prompt, x_sample, norm1_g, w_in, gate_b, gm_ln_g, gm_ln_b, gm_ws, gm_bs,
              sw_sink, cv_dw, cv_dwb, cv_ln_g, cv_ln_b, na_rpb, w_branch, w_out,
              norm2_g, w_up, ffn_dw, ffn_dwb, w_down, final_g):
    y_prompt = _trunk(x_prompt, norm1_g, w_in, gate_b, gm_ln_g, gm_ln_b, gm_ws, gm_bs, sw_sink,
                      cv_dw, cv_dwb, cv_ln_g, cv_ln_b, na_rpb, w_branch, w_out,
                      norm2_g, w_up, ffn_dw, ffn_dwb, w_down, final_g)
    y_sample = _trunk(x_sample, norm1_g, w_in, gate_b, gm_ln_g, gm_ln_b, gm_ws, gm_bs, sw_sink,
                      cv_dw, cv_dwb, cv_ln_g, cv_ln_b, na_rpb, w_branch, w_out,
                      norm2_g, w_up, ffn_dw, ffn_dwb, w_down, final_g)
    return (y_prompt, y_sample)
```

```python
import functools

import numpy as np
import jax
import jax.numpy as jnp
from jax import lax
from jax.experimental import pallas as pl
from jax.experimental.pallas import tpu as pltpu

F32 = jnp.float32
BF16 = jnp.bfloat16

D_MODEL = 2048
DEPTH = 2
GRID_W = 64
N_BRANCH = 4
BR_W = 512
HEAD_DIM = 64
CHUNK = 128
GM_GROUPS = 4
SW_HEADS = 8
SW_KV = 2
SW_WIN = 128
ROPE_THETA = 10000.0
CONV_W = 31
NA_HEADS = 8
NA_ROWS = 8
NA_COLS = 16
FFN_HIDDEN = 5632
EPS = 1e-6
NEG_INF = -1e30

LANES = 128
CONV_HALO = 16
FFN_HALO = 16

Z_GATES = 0
Z_GM_U = 8192
Z_GM_V = 8704
Z_SW_Q = 9216
Z_CV_A = 9728
Z_CV_G = 10240
Z_NA_Q = 10752
Z_NA_K = 11264
Z_NA_V = 11776
Z_SW_K = 12288
Z_SW_V = 12416
Z_COLS = 12544

VMEM_LIMIT = 56 * 1024 * 1024


def _params(*sem):
    return pltpu.CompilerParams(dimension_semantics=sem, vmem_limit_bytes=VMEM_LIMIT)


def _norm_proj_kernel(x_ref, g_ref, w_ref, o_ref, h_ref):
    @pl.when(pl.program_id(1) == 0)
    def _():
        x = x_ref[...]
        ms = jnp.mean(x * x, axis=-1, keepdims=True)
        h_ref[...] = (x * lax.rsqrt(ms + EPS) * g_ref[...]).astype(BF16)

    o_ref[...] = jnp.dot(h_ref[...], w_ref[...], preferred_element_type=F32).astype(o_ref.dtype)


def _norm_proj(x, g, w, *, tm, tn):
    n, d = x.shape
    nout = w.shape[1]
    return pl.pallas_call(
        _norm_proj_kernel,
        grid=(n // tm, nout // tn),
        in_specs=[
            pl.BlockSpec((tm, d), lambda i, j: (i, 0)),
            pl.BlockSpec((1, d), lambda i, j: (0, 0)),
            pl.BlockSpec((d, tn), lambda i, j: (0, j)),
        ],
        out_specs=pl.BlockSpec((tm, tn), lambda i, j: (i, j)),
        out_shape=jax.ShapeDtypeStruct((n, nout), BF16),
        scratch_shapes=[pltpu.VMEM((tm, d), BF16)],
        compiler_params=_params("parallel", "arbitrary"),
        name="norm_proj",
    )(x, g.reshape(1, d), w)


def _layernorm_rows(x, g, b):
    mu = jnp.mean(x, axis=-1, keepdims=True)
    xc = x - mu
    var = jnp.mean(xc * xc, axis=-1, keepdims=True)
    return xc * lax.rsqrt(var + EPS) * g + b


def _gmlp_kernel(u_ref, v_ref, lng_ref, lnb_ref, ws_ref, bs_ref, o_ref, *, tm):
    vn = _layernorm_rows(v_ref[...].astype(F32), lng_ref[...], lnb_ref[...]).astype(BF16)
    gw = BR_W // GM_GROUPS
    for c in range(tm // CHUNK):
        rows = slice(c * CHUNK, (c + 1) * CHUNK)
        for g in range(GM_GROUPS):
            cols = slice(g * gw, (g + 1) * gw)
            mixed = jnp.dot(ws_ref[g], vn[rows, cols], preferred_element_type=F32) + bs_ref[:, cols]
            o_ref[rows, cols] = (u_ref[rows, cols].astype(F32) * mixed).astype(o_ref.dtype)


def _gmlp(z, ln_g, ln_b, ws, bs, *, tm):
    n = z.shape[0]
    gw = BR_W // GM_GROUPS
    bs_full = jnp.repeat(bs.T, gw, axis=1)
    const = lambda i: (0, 0)
    return pl.pallas_call(
        functools.partial(_gmlp_kernel, tm=tm),
        grid=(n // tm,),
        in_specs=[
            pl.BlockSpec((tm, BR_W), lambda i: (i, Z_GM_U // BR_W)),
            pl.BlockSpec((tm, BR_W), lambda i: (i, Z_GM_V // BR_W)),
            pl.BlockSpec((1, BR_W), const),
            pl.BlockSpec((1, BR_W), const),
            pl.BlockSpec((GM_GROUPS, CHUNK, CHUNK), lambda i: (0, 0, 0)),
            pl.BlockSpec((CHUNK, BR_W), const),
        ],
        out_specs=pl.BlockSpec((tm, BR_W), lambda i: (i, 0)),
        out_shape=jax.ShapeDtypeStruct((n, BR_W), BF16),
        compiler_params=_params("parallel"),
        name="gmlp",
    )(z, z, ln_g.reshape(1, BR_W), ln_b.reshape(1, BR_W), ws.astype(BF16), bs_full)


def _rope(x, cos, sin, first_half):
    w = x.shape[1]
    reps = w // LANES
    if reps > 1:
        cos = jnp.concatenate([cos] * reps, axis=1)
        sin = jnp.concatenate([sin] * reps, axis=1)
        first_half = jnp.concatenate([first_half] * reps, axis=1)
    half = HEAD_DIM // 2
    lower = pltpu.roll(x, half, 1)
    upper = pltpu.roll(x, w - half, 1)
    return x * cos + jnp.where(first_half, -upper, lower) * sin


def _swa_kernel(sink_ref, q_ref, kp_ref, k_ref, kn_ref, vp_ref, v_ref, vn_ref, cos_ref, sin_ref,
                o_ref, qs_ref, kd_ref, vd_ref, *, tq, seq):
    t = pl.program_id(1)
    base = t * tq
    blk = SW_WIN
    lane = lax.broadcasted_iota(jnp.int32, (1, LANES), 1)
    first_half = (lane % HEAD_DIM) < (HEAD_DIM // 2)
    lane_lo = lane < HEAD_DIM

    def table(ref, start, size):
        return ref[pl.ds(pl.multiple_of(start, blk), size), :]

    p_start = jnp.maximum(base - blk, 0)
    n_start = jnp.minimum(base + tq, seq - blk)
    cq, sq = table(cos_ref, base, tq), table(sin_ref, base, tq)
    qs_ref[...] = _rope(q_ref[...].astype(F32), cq, sq, first_half).astype(BF16)
    k_ext = jnp.concatenate([
        _rope(kp_ref[...].astype(F32), table(cos_ref, p_start, blk), table(sin_ref, p_start, blk), first_half),
        _rope(k_ref[...].astype(F32), cq, sq, first_half),
        _rope(kn_ref[...].astype(F32), table(cos_ref, n_start, blk), table(sin_ref, n_start, blk), first_half),
    ], axis=0)
    v_ext = jnp.concatenate([vp_ref[...], v_ref[...], vn_ref[...]], axis=0).astype(F32)
    for src, dst in ((k_ext, kd_ref), (v_ext, vd_ref)):
        swapped = pltpu.roll(src, HEAD_DIM, 1)
        dst[0] = jnp.where(lane_lo, src, swapped).astype(BF16)
        dst[1] = jnp.where(lane_lo, swapped, src).astype(BF16)

    g = SW_HEADS // SW_KV
    qi_idx = lax.broadcasted_iota(jnp.int32, (blk, 3 * blk), 0)
    ki_idx = lax.broadcasted_iota(jnp.int32, (blk, 3 * blk), 1)
    band = (ki_idx >= qi_idx) & (ki_idx - qi_idx <= 2 * SW_WIN)
    for i in range(tq // blk):
        kpos = base + (i - 1) * blk + ki_idx
        valid = band & (kpos >= 0) & (kpos < seq)
        qi = qs_ref[i * blk:(i + 1) * blk, :]
        outs = []
        for j in range(SW_KV):
            kj = kd_ref[j, i * blk:(i + 3) * blk, :]
            vj = vd_ref[j, i * blk:(i + 3) * blk, :]
            parts = []
            for gi in range(g):
                h = g * j + gi
                qp = qi[:, (h // 2) * LANES:(h // 2 + 1) * LANES]
                keep = lane_lo if h % 2 == 0 else jnp.logical_not(lane_lo)
                parts.append(jnp.where(keep, qp, jnp.zeros_like(qp)))
            s_all = lax.dot_general(jnp.concatenate(parts, axis=0), kj, (((1,), (1,)), ((), ())),
                                    preferred_element_type=F32)
            ps, rden = [], []
            for gi in range(g):
                sink = sink_ref[g * j + gi]
                s = jnp.where(valid, s_all[gi * blk:(gi + 1) * blk], NEG_INF)
                m = jnp.maximum(jnp.max(s, axis=-1, keepdims=True), sink)
                p = jnp.exp(s - m)
                rden.append(1.0 / (jnp.sum(p, axis=-1, keepdims=True) + jnp.exp(sink - m)))
                ps.append(p.astype(BF16))
            o = jnp.dot(jnp.concatenate(ps, axis=0), vj, preferred_element_type=F32)
            outs.append(o * jnp.concatenate(rden, axis=0))
        for pr in range(SW_HEADS // 2):
            j, g0 = (2 * pr) // g, (2 * pr) % g
            pair = jnp.where(lane_lo, outs[j][g0 * blk:(g0 + 1) * blk], outs[j][(g0 + 1) * blk:(g0 + 2) * blk])
            o_ref[i * blk:(i + 1) * blk, pr * LANES:(pr + 1) * LANES] = pair.astype(o_ref.dtype)


def _rope_tables(seq):
    half = HEAD_DIM // 2
    inv = jnp.power(ROPE_THETA, -jnp.arange(half, dtype=F32) / half)
    ang = jnp.arange(seq, dtype=F32)[:, None] * inv[None, :]
    reps = LANES // half
    return jnp.tile(jnp.cos(ang), (1, reps)), jnp.tile(jnp.sin(ang), (1, reps))


def _swa(z, sink, *, batch, seq, tq):
    n = z.shape[0]
    blk = SW_WIN
    nt = seq // tq
    r = tq // blk
    cos_t, sin_t = _rope_tables(seq)
    kw = SW_KV * HEAD_DIM
    kcol, vcol = Z_SW_K // kw, Z_SW_V // kw
    nblk = n // blk

    def main(col):
        return lambda b, t: (b * nt + t, col)

    def prev(col):
        return lambda b, t: (jnp.maximum((b * nt + t) * r - 1, 0), col)

    def nxt(col):
        return lambda b, t: (jnp.minimum((b * nt + t + 1) * r, nblk - 1), col)

    const = lambda b, t: (0, 0)
    return pl.pallas_call(
        functools.partial(_swa_kernel, tq=tq, seq=seq),
        grid=(batch, nt),
        in_specs=[
            pl.BlockSpec(memory_space=pltpu.SMEM),
            pl.BlockSpec((tq, BR_W), main(Z_SW_Q // BR_W)),
            pl.BlockSpec((blk, kw), prev(kcol)),
            pl.BlockSpec((tq, kw), main(kcol)),
            pl.BlockSpec((blk, kw), nxt(kcol)),
            pl.BlockSpec((blk, kw), prev(vcol)),
            pl.BlockSpec((tq, kw), main(vcol)),
            pl.BlockSpec((blk, kw), nxt(vcol)),
            pl.BlockSpec((seq, LANES), const),
            pl.BlockSpec((seq, LANES), const),
        ],
        out_specs=pl.BlockSpec((tq, BR_W), lambda b, t: (b * nt + t, 0)),
        out_shape=jax.ShapeDtypeStruct((n, BR_W), BF16),
        scratch_shapes=[
            pltpu.VMEM((tq, BR_W), BF16),
            pltpu.VMEM((SW_KV, tq + 2 * blk, LANES), BF16),
            pltpu.VMEM((SW_KV, tq + 2 * blk, LANES), BF16),
        ],
        compiler_params=_params("parallel", "parallel"),
        name="swa",
    )(sink, z, z, z, z, z, z, z, cos_t, sin_t)


def _convmod_kernel(a_ref, g_ref, ap_ref, gp_ref, an_ref, gn_ref, dw_ref, dwb_ref, lng_ref, lnb_ref,
                    o_ref, xs_ref, *, tm, nt, rc):
    t = pl.program_id(1)
    hal = CONV_HALO

    def glu(a, g):
        return a[...].astype(F32) * jax.nn.sigmoid(g[...].astype(F32))

    xs_ref[hal:hal + tm, :] = glu(a_ref, g_ref)
    xs_ref[0:hal, :] = glu(ap_ref, gp_ref) * (t > 0).astype(F32)
    xs_ref[hal + tm:2 * hal + tm, :] = glu(an_ref, gn_ref) * (t < nt - 1).astype(F32)
    pad = CONV_W // 2
    for c in range(tm // rc):
        r0 = c * rc
        acc = jnp.zeros((rc, BR_W), F32)
        for k in range(CONV_W):
            off = hal - pad + k
            acc = acc + dw_ref[k:k + 1, :] * xs_ref[r0 + off:r0 + off + rc, :]
        y = _layernorm_rows(acc + dwb_ref[...], lng_ref[...], lnb_ref[...])
        o_ref[r0:r0 + rc, :] = (y * jax.nn.sigmoid(y)).astype(o_ref.dtype)


def _halo_specs(width, col, *, nt, tm, halo, nrows):
    r = tm // halo
    last = nrows // halo - 1
    return (
        pl.BlockSpec((tm, width), lambda b, t: (b * nt + t, col)),
        pl.BlockSpec((halo, width), lambda b, t: (jnp.maximum((b * nt + t) * r - 1, 0), col)),
        pl.BlockSpec((halo, width), lambda b, t: (jnp.minimum((b * nt + t + 1) * r, last), col)),
    )


def _convmod(z, dw, dwb, ln_g, ln_b, *, batch, seq, tm):
    n = z.shape[0]
    nt = seq // tm
    a_m, a_p, a_n = _halo_specs(BR_W, Z_CV_A // BR_W, nt=nt, tm=tm, halo=CONV_HALO, nrows=n)
    g_m, g_p, g_n = _halo_specs(BR_W, Z_CV_G // BR_W, nt=nt, tm=tm, halo=CONV_HALO, nrows=n)
    const = lambda b, t: (0, 0)
    row = lambda v: v.reshape(1, BR_W)
    return pl.pallas_call(
        functools.partial(_convmod_kernel, tm=tm, nt=nt, rc=64),
        grid=(batch, nt),
        in_specs=[a_m, g_m, a_p, g_p, a_n, g_n,
                  pl.BlockSpec((CONV_W, BR_W), const),
                  pl.BlockSpec((1, BR_W), const),
                  pl.BlockSpec((1, BR_W), const),
                  pl.BlockSpec((1, BR_W), const)],
        out_specs=pl.BlockSpec((tm, BR_W), lambda b, t: (b * nt + t, 0)),
        out_shape=jax.ShapeDtypeStruct((n, BR_W), BF16),
        scratch_shapes=[pltpu.VMEM((tm + 2 * CONV_HALO, BR_W), F32)],
        compiler_params=_params("parallel", "parallel"),
        name="convmod",
    )(z, z, z, z, z, z, dw, row(dwb), row(ln_g), row(ln_b))


def _na_kernel(q_ref, k_ref, v_ref, bias_ref, o_ref, *, rb, rows):
    lane = lax.broadcasted_iota(jnp.int32, (1, LANES), 1)
    lane_lo = lane < HEAD_DIM
    win = NA_ROWS * GRID_W
    row0 = pl.program_id(1) * rb

    def one_row(i, carry):
        r = row0 + i
        rs = jnp.clip(r - NA_ROWS // 2, 0, rows - NA_ROWS)
        off = rs - r + NA_ROWS - 1
        q_rows = pl.ds(pl.multiple_of(i * GRID_W, GRID_W), GRID_W)
        k_rows = pl.ds(pl.multiple_of(rs * GRID_W, GRID_W), win)
        for pr in range(NA_HEADS // 2):
            cols = slice(pr * LANES, (pr + 1) * LANES)
            qp = q_ref[q_rows, cols]
            kp = k_ref[k_rows, cols]
            vp = v_ref[k_rows, cols]
            halves = []
            for hh in range(2):
                h = 2 * pr + hh
                keep = lane_lo if hh == 0 else jnp.logical_not(lane_lo)
                qm = jnp.where(keep, qp, jnp.zeros_like(qp))
                s = lax.dot_general(qm, kp, (((1,), (1,)), ((), ())), preferred_element_type=F32)
                bias = jnp.concatenate([bias_ref[h, off + 2 * w] for w in range(NA_ROWS // 2)], axis=1)
                s = s + bias
                m = jnp.max(s, axis=-1, keepdims=True)
                p = jnp.exp(s - m)
                rden = 1.0 / jnp.sum(p, axis=-1, keepdims=True)
                halves.append(jnp.dot(p.astype(BF16), vp, preferred_element_type=F32) * rden)
            o_ref[q_rows, cols] = jnp.where(lane_lo, halves[0], halves[1]).astype(o_ref.dtype)
        return carry

    lax.fori_loop(0, rb, one_row, 0)


def _na_bias_table(rpb):
    c = np.arange(GRID_W)
    cs = np.clip(c - NA_COLS // 2, 0, GRID_W - NA_COLS)
    kc = np.arange(GRID_W)
    ok = (kc[None, :] >= cs[:, None]) & (kc[None, :] < cs[:, None] + NA_COLS)
    dc = np.clip(kc[None, :] - c[:, None], 1 - NA_COLS, NA_COLS - 1) + NA_COLS - 1
    full = jnp.where(ok[None, None], rpb.astype(F32)[:, :, dc], NEG_INF)
    return jnp.concatenate([full[:, :-1], full[:, 1:]], axis=-1)


def _na(z, rpb, *, batch, seq, rb):
    n = z.shape[0]
    rows = seq // GRID_W
    nrb = rows // rb
    table = _na_bias_table(rpb)
    return pl.pallas_call(
        functools.partial(_na_kernel, rb=rb, rows=rows),
        grid=(batch, nrb),
        in_specs=[
            pl.BlockSpec((rb * GRID_W, BR_W), lambda b, t: (b * nrb + t, Z_NA_Q // BR_W)),
            pl.BlockSpec((seq, BR_W), lambda b, t: (b, Z_NA_K // BR_W)),
            pl.BlockSpec((seq, BR_W), lambda b, t: (b, Z_NA_V // BR_W)),
            pl.BlockSpec(table.shape, lambda b, t: (0, 0, 0, 0)),
        ],
        out_specs=pl.BlockSpec((rb * GRID_W, BR_W), lambda b, t: (b * nrb + t, 0)),
        out_shape=jax.ShapeDtypeStruct((n, BR_W), BF16),
        compiler_params=_params("parallel", "arbitrary"),
        name="nattn",
    )(z, z, z, table)


def _merge_kernel(oa_ref, ob_ref, oc_ref, od_ref, ga_ref, gb_ref, gc_ref, gd_ref, bias_ref, w_ref, o_ref):
    acc = None
    for k, (o, g) in enumerate(((oa_ref, ga_ref), (ob_ref, gb_ref), (oc_ref, gc_ref), (od_ref, gd_ref))):
        y = jnp.dot(o[...], w_ref[k], preferred_element_type=F32)
        term = jax.nn.sigmoid(g[...].astype(F32) + bias_ref[k:k + 1, :]) * y
        acc = term if acc is None else acc + term
    o_ref[...] = acc.astype(o_ref.dtype)


def _merge(z, branches, gate_b, w_branch, *, tm, tn):
    n = z.shape[0]
    nn = D_MODEL // tn
    o_spec = pl.BlockSpec((tm, BR_W), lambda i, j: (i, 0))
    g_specs = [pl.BlockSpec((tm, tn), functools.partial(lambda i, j, k: (i, (Z_GATES + k * D_MODEL) // tn + j), k=k))
               for k in range(N_BRANCH)]
    return pl.pallas_call(
        _merge_kernel,
        grid=(n // tm, nn),
        in_specs=[o_spec] * N_BRANCH + g_specs + [
            pl.BlockSpec((N_BRANCH, tn), lambda i, j: (0, j)),
            pl.BlockSpec((N_BRANCH, BR_W, tn), lambda i, j: (0, 0, j)),
        ],
        out_specs=pl.BlockSpec((tm, tn), lambda i, j: (i, j)),
        out_shape=jax.ShapeDtypeStruct((n, D_MODEL), BF16),
        compiler_params=_params("parallel", "arbitrary"),
        name="merge",
    )(*branches, z, z, z, z, gate_b, w_branch)


def _proj_res_kernel(a_ref, w_ref, x_ref, o_ref):
    o_ref[...] = x_ref[...] + jnp.dot(a_ref[...], w_ref[...], preferred_element_type=F32)


def _proj_res(a, w, x, *, tm, tn):
    n, k = a.shape
    nout = w.shape[1]
    return pl.pallas_call(
        _proj_res_kernel,
        grid=(n // tm, nout // tn),
        in_specs=[
            pl.BlockSpec((tm, k), lambda i, j: (i, 0)),
            pl.BlockSpec((k, tn), lambda i, j: (0, j)),
            pl.BlockSpec((tm, tn), lambda i, j: (i, j)),
        ],
        out_specs=pl.BlockSpec((tm, tn), lambda i, j: (i, j)),
        out_shape=jax.ShapeDtypeStruct((n, nout), F32),
        compiler_params=_params("parallel", "arbitrary"),
        name="proj_res",
    )(a, w, x)


def _ffn_kernel(x_ref, xp_ref, xn_ref, g_ref, wa_ref, wb_ref, dwa_ref, dwb_ref, ba_ref, bb_ref, wd_ref, fg_ref,
                o_ref, h_ref, acc_ref, *, tm, nt, nh, final_norm):
    t = pl.program_id(1)
    j = pl.program_id(2)
    hal = FFN_HALO

    def norm(x):
        ms = jnp.mean(x * x, axis=-1, keepdims=True)
        return x * lax.rsqrt(ms + EPS) * g_ref[...]

    @pl.when(j == 0)
    def _():
        h_ref[hal:hal + tm, :] = norm(x_ref[...]).astype(BF16)
        h_ref[0:hal, :] = (norm(xp_ref[...]) * (t > 0).astype(F32)).astype(BF16)
        h_ref[hal + tm:2 * hal + tm, :] = (norm(xn_ref[...]) * (t < nt - 1).astype(F32)).astype(BF16)

    h = h_ref[...]

    def conv(w_ref, dw_ref, b_ref):
        u = jnp.dot(h, w_ref[...], preferred_element_type=F32)
        rows = u.shape[0]
        below = pltpu.roll(u, 1, 0)[hal:hal + tm]
        above = pltpu.roll(u, rows - 1, 0)[hal:hal + tm]
        return dw_ref[0:1, :] * below + dw_ref[1:2, :] * u[hal:hal + tm] + dw_ref[2:3, :] * above + b_ref[...]

    a = conv(wa_ref, dwa_ref, ba_ref)
    b = conv(wb_ref, dwb_ref, bb_ref)
    gated = (a * jax.nn.sigmoid(a) * b).astype(BF16)
    part = jnp.dot(gated, wd_ref[...], preferred_element_type=F32)

    @pl.when(j == 0)
    def _():
        acc_ref[...] = part

    @pl.when(j > 0)
    def _():
        acc_ref[...] += part

    @pl.when(j == nh - 1)
    def _():
        y = x_ref[...] + acc_ref[...]
        if final_norm:
            ms = jnp.mean(y * y, axis=-1, keepdims=True)
            y = y * lax.rsqrt(ms + EPS) * fg_ref[...]
        o_ref[...] = y


def _ffn(x, norm_g, w_up, dw, dwb, w_down, final_g, *, batch, seq, tm, hb, final_norm):
    n, d = x.shape
    nt = seq // tm
    nh = FFN_HIDDEN // hb
    r = tm // FFN_HALO
    last = n // FFN_HALO - 1
    const = lambda b, t, j: (0, 0)
    first = lambda b, t, j: (0, j)
    second = lambda b, t, j: (0, nh + j)
    return pl.pallas_call(
        functools.partial(_ffn_kernel, tm=tm, nt=nt, nh=nh, final_norm=final_norm),
        grid=(batch, nt, nh),
        in_specs=[
            pl.BlockSpec((tm, d), lambda b, t, j: (b * nt + t, 0)),
            pl.BlockSpec((FFN_HALO, d), lambda b, t, j: (jnp.maximum((b * nt + t) * r - 1, 0), 0)),
            pl.BlockSpec((FFN_HALO, d), lambda b, t, j: (jnp.minimum((b * nt + t + 1) * r, last), 0)),
            pl.BlockSpec((1, d), const),
            pl.BlockSpec((d, hb), first),
            pl.BlockSpec((d, hb), second),
            pl.BlockSpec((3, hb), first),
            pl.BlockSpec((3, hb), second),
            pl.BlockSpec((1, hb), first),
            pl.BlockSpec((1, hb), second),
            pl.BlockSpec((hb, d), lambda b, t, j: (j, 0)),
            pl.BlockSpec((1, d), const),
        ],
        out_specs=pl.BlockSpec((tm, d), lambda b, t, j: (b * nt + t, 0)),
        out_shape=jax.ShapeDtypeStruct((n, d), F32),
        scratch_shapes=[pltpu.VMEM((tm + 2 * FFN_HALO, d), BF16), pltpu.VMEM((tm, d), F32)],
        compiler_params=_params("parallel", "parallel", "arbitrary"),
        name="conv_ffn",
    )(x, x, x, norm_g.reshape(1, d), w_up, w_up, dw, dw, dwb.reshape(1, -1), dwb.reshape(1, -1), w_down,
      final_g.reshape(1, d))


def _permute_w_in(w_in):
    sizes = (BR_W, BR_W, SW_HEADS * HEAD_DIM, SW_KV * HEAD_DIM, SW_KV * HEAD_DIM, BR_W, BR_W,
             NA_HEADS * HEAD_DIM, NA_HEADS * HEAD_DIM, NA_HEADS * HEAD_DIM, N_BRANCH * D_MODEL)
    offs = np.concatenate([[0], np.cumsum(sizes)])
    (gm_u, gm_v, sw_q, sw_k, sw_v, cv_a, cv_g, na_q, na_k, na_v, gates) = [
        w_in[:, :, offs[i]:offs[i + 1]] for i in range(len(sizes))]
    scale = HEAD_DIM ** -0.5
    parts = [gates, gm_u, gm_v, sw_q * scale, cv_a, cv_g, na_q * scale, na_k, na_v, sw_k, sw_v]
    return jnp.concatenate(parts, axis=-1).astype(BF16)


def _trunk(x, p, *, tm=512):
    batch, seq, d = x.shape
    n = batch * seq
    x2 = x.reshape(n, d)
    for l in range(DEPTH):
        z = _norm_proj(x2, p["norm1_g"][l], p["w_in"][l], tm=tm, tn=1792)
        o_a = _gmlp(z, p["gm_ln_g"][l], p["gm_ln_b"][l], p["gm_ws"][l], p["gm_bs"][l], tm=tm)
        o_b = _swa(z, p["sw_sink"][l], batch=batch, seq=seq, tq=tm)
        o_c = _convmod(z, p["cv_dw"][l], p["cv_dwb"][l], p["cv_ln_g"][l], p["cv_ln_b"][l],
                       batch=batch, seq=seq, tm=tm)
        o_d = _na(z, p["na_rpb"][l], batch=batch, seq=seq, rb=8)
        merged = _merge(z, (o_a, o_b, o_c, o_d), p["gate_b"][l], p["w_branch"][l], tm=tm, tn=512)
        x2 = _proj_res(merged, p["w_out"][l], x2, tm=tm, tn=1024)
        x2 = _ffn(x2, p["norm2_g"][l], p["w_up"][l], p["ffn_dw"][l], p["ffn_dwb"][l], p["w_down"][l],
                  p["final_g"], batch=batch, seq=seq, tm=tm, hb=512, final_norm=(l == DEPTH - 1))
    return x2.reshape(batch, seq, d)


def kernel(x_prompt, x_sample, norm1_g, w_in, gate_b, gm_ln_g, gm_ln_b, gm_ws, gm_bs, sw_sink, cv_dw, cv_dwb,
           cv_ln_g, cv_ln_b, na_rpb, w_branch, w_out, norm2_g, w_up, ffn_dw, ffn_dwb, w_down, final_g):
    p = dict(norm1_g=norm1_g, w_in=_permute_w_in(w_in), gate_b=gate_b, gm_ln_g=gm_ln_g, gm_ln_b=gm_ln_b,
             gm_ws=gm_ws, gm_bs=gm_bs, sw_sink=sw_sink, cv_dw=cv_dw, cv_dwb=cv_dwb, cv_ln_g=cv_ln_g,
             cv_ln_b=cv_ln_b, na_rpb=na_rpb, w_branch=w_branch.astype(BF16), w_out=w_out.astype(BF16),
             norm2_g=norm2_g, w_up=w_up.astype(BF16), ffn_dw=ffn_dw, ffn_dwb=ffn_dwb,
             w_down=w_down.astype(BF16), final_g=final_g)
    return (_trunk(x_prompt, p), _trunk(x_sample, p))
```

```python
import functools

import numpy as np
import jax
import jax.numpy as jnp
from jax import lax
from jax.experimental import pallas as pl
from jax.experimental.pallas import tpu as pltpu

F32 = jnp.float32
BF16 = jnp.bfloat16

D_MODEL = 2048
DEPTH = 2
GRID_W = 64
N_BRANCH = 4
BR_W = 512
HEAD_DIM = 64
CHUNK = 128
GM_GROUPS = 4
SW_HEADS = 8
SW_KV = 2
SW_WIN = 128
ROPE_THETA = 10000.0
CONV_W = 31
NA_HEADS = 8
NA_ROWS = 8
NA_COLS = 16
NA_QROWS = 4
NA_KROWS = 12
NA_T_LOW = 2 * NA_ROWS - 2
NA_T_HIGH = NA_T_LOW + NA_ROWS
NA_T_NONE = NA_T_HIGH + NA_ROWS
FFN_HIDDEN = 5632
EPS = 1e-6
NEG_INF = -1e30

LANES = 128
SUBLANES = 8
MXU_DIM = 256
CONV_HALO = 16
FFN_HALO = SUBLANES

Z_GATES = 0
Z_GM_U = 8192
Z_GM_V = 8704
Z_SW_Q = 9216
Z_CV_A = 9728
Z_CV_G = 10240
Z_NA_Q = 10752
Z_NA_K = 11264
Z_NA_V = 11776
Z_SW_K = 12288
Z_SW_V = 12416
Z_COLS = 12544

VMEM_LIMIT = 56 * 1024 * 1024


def _params(*sem, flags=None):
    return pltpu.CompilerParams(dimension_semantics=sem, vmem_limit_bytes=VMEM_LIMIT, flags=flags)


def _norm_proj_kernel(x_ref, g_ref, w_ref, o_ref, h_ref):
    @pl.when(pl.program_id(1) == 0)
    def _():
        x = x_ref[...]
        ms = jnp.mean(x * x, axis=-1, keepdims=True)
        h_ref[...] = (x * lax.rsqrt(ms + EPS) * g_ref[...]).astype(BF16)

    o_ref[...] = jnp.dot(h_ref[...], w_ref[...], preferred_element_type=F32).astype(o_ref.dtype)


def _norm_proj(x, g, w, *, tm, tn):
    n, d = x.shape
    nout = w.shape[1]
    return pl.pallas_call(
        _norm_proj_kernel,
        grid=(n // tm, nout // tn),
        in_specs=[
            pl.BlockSpec((tm, d), lambda i, j: (i, 0)),
            pl.BlockSpec((1, d), lambda i, j: (0, 0)),
            pl.BlockSpec((d, tn), lambda i, j: (0, j)),
        ],
        out_specs=pl.BlockSpec((tm, tn), lambda i, j: (i, j)),
        out_shape=jax.ShapeDtypeStruct((n, nout), BF16),
        scratch_shapes=[pltpu.VMEM((tm, d), BF16)],
        compiler_params=_params("parallel", "arbitrary"),
        name="norm_proj",
    )(x, g.reshape(1, d), w)


def _layernorm_rows(x, g, b):
    mu = jnp.mean(x, axis=-1, keepdims=True)
    xc = x - mu
    var = jnp.mean(xc * xc, axis=-1, keepdims=True)
    return xc * lax.rsqrt(var + EPS) * g + b


def _gmlp_kernel(u_ref, v_ref, lng_ref, lnb_ref, ws_ref, bs_ref, o_ref, *, tm):
    vn = _layernorm_rows(v_ref[...].astype(F32), lng_ref[...], lnb_ref[...]).astype(BF16)
    gw = BR_W // GM_GROUPS
    for c in range(tm // CHUNK):
        rows = slice(c * CHUNK, (c + 1) * CHUNK)
        for g in range(GM_GROUPS):
            cols = slice(g * gw, (g + 1) * gw)
            mixed = jnp.dot(ws_ref[g], vn[rows, cols], preferred_element_type=F32) + bs_ref[:, cols]
            o_ref[rows, cols] = (u_ref[rows, cols].astype(F32) * mixed).astype(o_ref.dtype)


def _gmlp(z, ln_g, ln_b, ws, bs, *, tm):
    n = z.shape[0]
    gw = BR_W // GM_GROUPS
    bs_full = jnp.repeat(bs.T, gw, axis=1)
    const = lambda i: (0, 0)
    return pl.pallas_call(
        functools.partial(_gmlp_kernel, tm=tm),
        grid=(n // tm,),
        in_specs=[
            pl.BlockSpec((tm, BR_W), lambda i: (i, Z_GM_U // BR_W)),
            pl.BlockSpec((tm, BR_W), lambda i: (i, Z_GM_V // BR_W)),
            pl.BlockSpec((1, BR_W), const),
            pl.BlockSpec((1, BR_W), const),
            pl.BlockSpec((GM_GROUPS, CHUNK, CHUNK), lambda i: (0, 0, 0)),
            pl.BlockSpec((CHUNK, BR_W), const),
        ],
        out_specs=pl.BlockSpec((tm, BR_W), lambda i: (i, 0)),
        out_shape=jax.ShapeDtypeStruct((n, BR_W), BF16),
        compiler_params=_params("parallel"),
        name="gmlp",
    )(z, z, ln_g.reshape(1, BR_W), ln_b.reshape(1, BR_W), ws.astype(BF16), bs_full)


def _rope(x, cos, sin, first_half):
    w = x.shape[1]
    reps = w // LANES
    if reps > 1:
        cos = jnp.concatenate([cos] * reps, axis=1)
        sin = jnp.concatenate([sin] * reps, axis=1)
        first_half = jnp.concatenate([first_half] * reps, axis=1)
    half = HEAD_DIM // 2
    lower = pltpu.roll(x, half, 1)
    upper = pltpu.roll(x, w - half, 1)
    return x * cos + jnp.where(first_half, -upper, lower) * sin


def _swa_kernel(sink_ref, q_ref, kp_ref, k_ref, kn_ref, vp_ref, v_ref, vn_ref, cos_ref, sin_ref,
                o_ref, qs_ref, kd_ref, vd_ref, *, tq, seq):
    t = pl.program_id(1)
    base = t * tq
    blk = SW_WIN
    lane = lax.broadcasted_iota(jnp.int32, (1, LANES), 1)
    first_half = (lane % HEAD_DIM) < (HEAD_DIM // 2)
    lane_lo = lane < HEAD_DIM

    def table(ref, start, size):
        return ref[pl.ds(pl.multiple_of(start, blk), size), :]

    p_start = jnp.maximum(base - blk, 0)
    n_start = jnp.minimum(base + tq, seq - blk)
    cq, sq = table(cos_ref, base, tq), table(sin_ref, base, tq)
    qs_ref[...] = _rope(q_ref[...].astype(F32), cq, sq, first_half).astype(BF16)
    k_ext = jnp.concatenate([
        _rope(kp_ref[...].astype(F32), table(cos_ref, p_start, blk), table(sin_ref, p_start, blk), first_half),
        _rope(k_ref[...].astype(F32), cq, sq, first_half),
        _rope(kn_ref[...].astype(F32), table(cos_ref, n_start, blk), table(sin_ref, n_start, blk), first_half),
    ], axis=0)
    v_ext = jnp.concatenate([vp_ref[...], v_ref[...], vn_ref[...]], axis=0).astype(F32)
    for src, dst in ((k_ext, kd_ref), (v_ext, vd_ref)):
        swapped = pltpu.roll(src, HEAD_DIM, 1)
        dst[0] = jnp.where(lane_lo, src, swapped).astype(BF16)
        dst[1] = jnp.where(lane_lo, swapped, src).astype(BF16)

    g = SW_HEADS // SW_KV
    qi_idx = lax.broadcasted_iota(jnp.int32, (blk, 3 * blk), 0)
    ki_idx = lax.broadcasted_iota(jnp.int32, (blk, 3 * blk), 1)
    band = (ki_idx >= qi_idx) & (ki_idx - qi_idx <= 2 * SW_WIN)
    for i in range(tq // blk):
        kpos = base + (i - 1) * blk + ki_idx
        valid = band & (kpos >= 0) & (kpos < seq)
        qi = qs_ref[i * blk:(i + 1) * blk, :]
        outs = []
        for j in range(SW_KV):
            kj = kd_ref[j, i * blk:(i + 3) * blk, :]
            vj = vd_ref[j, i * blk:(i + 3) * blk, :]
            parts = []
            for gi in range(g):
                h = g * j + gi
                qp = qi[:, (h // 2) * LANES:(h // 2 + 1) * LANES]
                keep = lane_lo if h % 2 == 0 else jnp.logical_not(lane_lo)
                parts.append(jnp.where(keep, qp, jnp.zeros_like(qp)))
            s_all = lax.dot_general(jnp.concatenate(parts, axis=0), kj, (((1,), (1,)), ((), ())),
                                    preferred_element_type=F32)
            ps, rden = [], []
            for gi in range(g):
                sink = sink_ref[g * j + gi]
                s = jnp.where(valid, s_all[gi * blk:(gi + 1) * blk], NEG_INF)
                m = jnp.maximum(jnp.max(s, axis=-1, keepdims=True), sink)
                p = jnp.exp(s - m)
                rden.append(1.0 / (jnp.sum(p, axis=-1, keepdims=True) + jnp.exp(sink - m)))
                ps.append(p.astype(BF16))
            o = jnp.dot(jnp.concatenate(ps, axis=0), vj, preferred_element_type=F32)
            outs.append(o * jnp.concatenate(rden, axis=0))
        for pr in range(SW_HEADS // 2):
            j, g0 = (2 * pr) // g, (2 * pr) % g
            pair = jnp.where(lane_lo, outs[j][g0 * blk:(g0 + 1) * blk], outs[j][(g0 + 1) * blk:(g0 + 2) * blk])
            o_ref[i * blk:(i + 1) * blk, pr * LANES:(pr + 1) * LANES] = pair.astype(o_ref.dtype)


def _rope_tables(seq):
    half = HEAD_DIM // 2
    inv = jnp.power(ROPE_THETA, -jnp.arange(half, dtype=F32) / half)
    ang = jnp.arange(seq, dtype=F32)[:, None] * inv[None, :]
    reps = LANES // half
    return jnp.tile(jnp.cos(ang), (1, reps)), jnp.tile(jnp.sin(ang), (1, reps))


def _swa(z, sink, *, batch, seq, tq):
    n = z.shape[0]
    blk = SW_WIN
    nt = seq // tq
    r = tq // blk
    cos_t, sin_t = _rope_tables(seq)
    kw = SW_KV * HEAD_DIM
    kcol, vcol = Z_SW_K // kw, Z_SW_V // kw
    nblk = n // blk

    def main(col):
        return lambda b, t: (b * nt + t, col)

    def prev(col):
        return lambda b, t: (jnp.maximum((b * nt + t) * r - 1, 0), col)

    def nxt(col):
        return lambda b, t: (jnp.minimum((b * nt + t + 1) * r, nblk - 1), col)

    const = lambda b, t: (0, 0)
    return pl.pallas_call(
        functools.partial(_swa_kernel, tq=tq, seq=seq),
        grid=(batch, nt),
        in_specs=[
            pl.BlockSpec(memory_space=pltpu.SMEM),
            pl.BlockSpec((tq, BR_W), main(Z_SW_Q // BR_W)),
            pl.BlockSpec((blk, kw), prev(kcol)),
            pl.BlockSpec((tq, kw), main(kcol)),
            pl.BlockSpec((blk, kw), nxt(kcol)),
            pl.BlockSpec((blk, kw), prev(vcol)),
            pl.BlockSpec((tq, kw), main(vcol)),
            pl.BlockSpec((blk, kw), nxt(vcol)),
            pl.BlockSpec((seq, LANES), const),
            pl.BlockSpec((seq, LANES), const),
        ],
        out_specs=pl.BlockSpec((tq, BR_W), lambda b, t: (b * nt + t, 0)),
        out_shape=jax.ShapeDtypeStruct((n, BR_W), BF16),
        scratch_shapes=[
            pltpu.VMEM((tq, BR_W), BF16),
            pltpu.VMEM((SW_KV, tq + 2 * blk, LANES), BF16),
            pltpu.VMEM((SW_KV, tq + 2 * blk, LANES), BF16),
        ],
        compiler_params=_params("parallel", "parallel"),
        name="swa",
    )(sink, z, z, z, z, z, z, z, cos_t, sin_t)


def _convmod_kernel(a_ref, g_ref, ap_ref, gp_ref, an_ref, gn_ref, dw_ref, dwb_ref, lng_ref, lnb_ref,
                    o_ref, xs_ref, *, tm, nt, rc):
    t = pl.program_id(1)
    hal = CONV_HALO
    rows = tm + 2 * hal

    def glu(a, g):
        return a[...].astype(F32) * jax.nn.sigmoid(g[...].astype(F32))

    xs_ref[0, 0:hal, :] = glu(ap_ref, gp_ref) * (t > 0).astype(F32)
    xs_ref[0, hal:hal + tm, :] = glu(a_ref, g_ref)
    xs_ref[0, hal + tm:rows, :] = glu(an_ref, gn_ref) * (t < nt - 1).astype(F32)
    for r0 in range(0, rows - SUBLANES, rc):
        nr = min(rc, rows - SUBLANES - r0)
        x = xs_ref[0, r0:r0 + nr + SUBLANES, :]
        for s in range(1, SUBLANES):
            xs_ref[s, r0:r0 + nr, :] = pltpu.roll(x, nr + SUBLANES - s, 0)[0:nr]
    pad = CONV_W // 2
    for c in range(tm // rc):
        r0 = c * rc
        acc = jnp.zeros((rc, BR_W), F32)
        for k in range(CONV_W):
            off = hal - pad + k
            al = r0 + (off // SUBLANES) * SUBLANES
            acc = acc + dw_ref[k:k + 1, :] * xs_ref[off % SUBLANES, al:al + rc, :]
        y = _layernorm_rows(acc + dwb_ref[...], lng_ref[...], lnb_ref[...])
        o_ref[r0:r0 + rc, :] = (y * jax.nn.sigmoid(y)).astype(o_ref.dtype)


def _halo_specs(width, col, *, nt, tm, halo, nrows):
    r = tm // halo
    last = nrows // halo - 1
    return (
        pl.BlockSpec((tm, width), lambda b, t: (b * nt + t, col)),
        pl.BlockSpec((halo, width), lambda b, t: (jnp.maximum((b * nt + t) * r - 1, 0), col)),
        pl.BlockSpec((halo, width), lambda b, t: (jnp.minimum((b * nt + t + 1) * r, last), col)),
    )


def _convmod(z, dw, dwb, ln_g, ln_b, *, batch, seq, tm):
    n = z.shape[0]
    nt = seq // tm
    a_m, a_p, a_n = _halo_specs(BR_W, Z_CV_A // BR_W, nt=nt, tm=tm, halo=CONV_HALO, nrows=n)
    g_m, g_p, g_n = _halo_specs(BR_W, Z_CV_G // BR_W, nt=nt, tm=tm, halo=CONV_HALO, nrows=n)
    const = lambda b, t: (0, 0)
    row = lambda v: v.reshape(1, BR_W)
    return pl.pallas_call(
        functools.partial(_convmod_kernel, tm=tm, nt=nt, rc=64),
        grid=(batch, nt),
        in_specs=[a_m, g_m, a_p, g_p, a_n, g_n,
                  pl.BlockSpec((CONV_W, BR_W), const),
                  pl.BlockSpec((1, BR_W), const),
                  pl.BlockSpec((1, BR_W), const),
                  pl.BlockSpec((1, BR_W), const)],
        out_specs=pl.BlockSpec((tm, BR_W), lambda b, t: (b * nt + t, 0)),
        out_shape=jax.ShapeDtypeStruct((n, BR_W), BF16),
        scratch_shapes=[pltpu.VMEM((SUBLANES, tm + 2 * CONV_HALO, BR_W), F32)],
        compiler_params=_params("parallel", "parallel"),
        name="convmod",
    )(z, z, z, z, z, z, dw, row(dwb), row(ln_g), row(ln_b))


def _na_kernel(q_ref, k_ref, v_ref, bias_ref, o_ref, *, rb, rows):
    lane = lax.broadcasted_iota(jnp.int32, (1, LANES), 1)
    lane_lo = lane < HEAD_DIM
    row0 = pl.program_id(1) * rb
    npair = NA_KROWS // 2

    def one_group(gi, carry):
        r0 = row0 + gi * NA_QROWS
        ks = jnp.clip(r0 - NA_ROWS // 2, 0, rows - NA_KROWS)
        q_rows = pl.ds(pl.multiple_of(gi * (NA_QROWS * GRID_W), NA_QROWS * GRID_W), NA_QROWS * GRID_W)
        k_rows = pl.ds(pl.multiple_of(ks * GRID_W, GRID_W), NA_KROWS * GRID_W)
        entry = []
        for qi in range(NA_QROWS):
            r = r0 + qi
            rs = jnp.clip(r - NA_ROWS // 2, 0, rows - NA_ROWS)
            for w in range(npair):
                kr = ks + 2 * w
                d = kr - r + NA_ROWS - 1
                in0 = (kr >= rs) & (kr < rs + NA_ROWS)
                in1 = (kr + 1 >= rs) & (kr + 1 < rs + NA_ROWS)
                entry.append(jnp.where(in0 & in1, d,
                                       jnp.where(in1, NA_T_LOW + d + 1,
                                                 jnp.where(in0, NA_T_HIGH + d - (NA_ROWS - 1), NA_T_NONE))))
        for pr in range(NA_HEADS // 2):
            cols = slice(pr * LANES, (pr + 1) * LANES)
            qp = q_ref[q_rows, cols]
            kp = k_ref[k_rows, cols]
            vp = v_ref[k_rows, cols]
            halves = []
            for hh in range(2):
                h = 2 * pr + hh
                keep = lane_lo if hh == 0 else jnp.logical_not(lane_lo)
                qm = jnp.where(keep, qp, jnp.zeros_like(qp))
                s = lax.dot_general(qm, kp, (((1,), (1,)), ((), ())), preferred_element_type=F32)
                bias = jnp.concatenate([
                    jnp.concatenate([bias_ref[h, entry[qi * npair + w]] for w in range(npair)], axis=1)
                    for qi in range(NA_QROWS)], axis=0)
                s = s + bias
                m = jnp.max(s, axis=-1, keepdims=True)
                p = jnp.exp(s - m)
                rden = 1.0 / jnp.sum(p, axis=-1, keepdims=True)
                halves.append(jnp.dot(p.astype(BF16), vp, preferred_element_type=F32) * rden)
            o_ref[q_rows, cols] = jnp.where(lane_lo, halves[0], halves[1]).astype(o_ref.dtype)
        return carry

    lax.fori_loop(0, rb // NA_QROWS, one_group, 0)


def _na_bias_table(rpb):
    c = np.arange(GRID_W)
    cs = np.clip(c - NA_COLS // 2, 0, GRID_W - NA_COLS)
    kc = np.arange(GRID_W)
    ok = (kc[None, :] >= cs[:, None]) & (kc[None, :] < cs[:, None] + NA_COLS)
    dc = np.clip(kc[None, :] - c[:, None], 1 - NA_COLS, NA_COLS - 1) + NA_COLS - 1
    full = jnp.where(ok[None, None], rpb.astype(F32)[:, :, dc], NEG_INF)
    out = jnp.full_like(full[:, :NA_ROWS], NEG_INF)
    return jnp.concatenate([
        jnp.concatenate([full[:, :-1], full[:, 1:]], axis=-1),
        jnp.concatenate([out, full[:, :NA_ROWS]], axis=-1),
        jnp.concatenate([full[:, NA_ROWS - 1:], out], axis=-1),
        jnp.concatenate([out[:, :1], out[:, :1]], axis=-1)], axis=1)


def _na(z, table, *, batch, seq, rb):
    n = z.shape[0]
    rows = seq // GRID_W
    nrb = rows // rb
    return pl.pallas_call(
        functools.partial(_na_kernel, rb=rb, rows=rows),
        grid=(batch, nrb),
        in_specs=[
            pl.BlockSpec((rb * GRID_W, BR_W), lambda b, t: (b * nrb + t, Z_NA_Q // BR_W)),
            pl.BlockSpec((seq, BR_W), lambda b, t: (b, Z_NA_K // BR_W)),
            pl.BlockSpec((seq, BR_W), lambda b, t: (b, Z_NA_V // BR_W)),
            pl.BlockSpec(table.shape, lambda b, t: (0, 0, 0, 0), pipeline_mode=pl.Buffered(1)),
        ],
        out_specs=pl.BlockSpec((rb * GRID_W, BR_W), lambda b, t: (b * nrb + t, 0)),
        out_shape=jax.ShapeDtypeStruct((n, BR_W), BF16),
        compiler_params=_params("parallel", "arbitrary"),
        name="nattn",
    )(z, z, z, table)


def _merge_proj_kernel(oa_ref, ob_ref, oc_ref, od_ref, g_ref, bias_ref, wb_ref, wo_ref, x_ref, o_ref):
    k = pl.program_id(1)
    tm = o_ref.shape[0]
    for kk, br_ref in enumerate((oa_ref, ob_ref, oc_ref, od_ref)):
        @pl.when(k == kk)
        def _(kk=kk, br_ref=br_ref):
            for r0 in range(0, tm, MXU_DIM):
                rows = slice(r0, r0 + MXU_DIM)
                y = jnp.dot(br_ref[rows, :], wb_ref[kk], preferred_element_type=F32)
                th = jnp.tanh(0.5 * (g_ref[rows, :].astype(F32) + bias_ref[kk:kk + 1, :]))
                term = (0.5 * y) * (th + 1.0)
                if kk == 0:
                    o_ref[rows, :] = term
                elif kk < N_BRANCH - 1:
                    o_ref[rows, :] += term
                else:
                    merged = (o_ref[rows, :] + term).astype(BF16)
                    o_ref[rows, :] = x_ref[rows, :] + jnp.dot(merged, wo_ref[...], preferred_element_type=F32)


def _merge_proj(z, branches, gate_b, w_branch, w_out, x, *, tm):
    n, d = x.shape
    br_spec = pl.BlockSpec((tm, BR_W), lambda i, k: (i, 0))
    return pl.pallas_call(
        _merge_proj_kernel,
        grid=(n // tm, N_BRANCH),
        in_specs=[br_spec] * N_BRANCH + [
            pl.BlockSpec((tm, d), lambda i, k: (i, Z_GATES // d + k)),
            pl.BlockSpec((N_BRANCH, d), lambda i, k: (0, 0)),
            pl.BlockSpec((N_BRANCH, BR_W, d), lambda i, k: (0, 0, 0), pipeline_mode=pl.Buffered(1)),
            pl.BlockSpec((d, d), lambda i, k: (0, 0), pipeline_mode=pl.Buffered(1)),
            pl.BlockSpec((tm, d), lambda i, k: (i, 0)),
        ],
        out_specs=pl.BlockSpec((tm, d), lambda i, k: (i, 0)),
        out_shape=jax.ShapeDtypeStruct((n, d), F32),
        compiler_params=_params("parallel", "arbitrary"),
        name="merge_proj",
    )(*branches, z, gate_b, w_branch, w_out, x)


def _ffn_kernel(x_ref, xp_ref, xn_ref, g_ref, wu_ref, dw_ref, b_ref, wd_ref, fg_ref,
                o_ref, h_ref, u0_ref, u1_ref, *, tm, nt, nh, hb, final_norm):
    t = pl.program_id(1)
    j = pl.program_id(2)
    rows = tm + 2 * FFN_HALO
    u_stage = (u0_ref, u1_ref)

    def norm(x):
        ms = jnp.mean(x * x, axis=-1, keepdims=True)
        return x * lax.rsqrt(ms + EPS) * g_ref[...]

    def up(dst_ref):
        h = h_ref[...]
        for c in range(hb // MXU_DIM):
            cols = slice(2 * c * MXU_DIM, 2 * (c + 1) * MXU_DIM)
            dst_ref[:, cols] = jnp.dot(h, wu_ref[:, cols], preferred_element_type=F32)

    def down(src_ref):
        part = None
        for c in range(hb // MXU_DIM):
            cols = slice(2 * c * MXU_DIM, 2 * (c + 1) * MXU_DIM)
            u = src_ref[:, cols]
            below = pltpu.roll(u, 1, 0)[0:tm]
            above = pltpu.roll(u, rows - 1, 0)[0:tm]
            ab = (dw_ref[0:1, cols] * below + dw_ref[1:2, cols] * u[0:tm] + dw_ref[2:3, cols] * above
                  + b_ref[:, cols])
            a, b = ab[:, :MXU_DIM], ab[:, MXU_DIM:]
            gated = ((a * b) * (0.5 * jnp.tanh(0.5 * a) + 0.5)).astype(BF16)
            pc = jnp.dot(gated, wd_ref[c * MXU_DIM:(c + 1) * MXU_DIM, :], preferred_element_type=F32)
            part = pc if part is None else part + pc
        o_ref[...] += part

    @pl.when(j == 0)
    def _():
        x = x_ref[...]
        h_ref[0:tm, :] = norm(x).astype(BF16)
        h_ref[tm:rows, :] = jnp.concatenate([
            norm(xn_ref[...]) * (t < nt - 1).astype(F32),
            norm(xp_ref[...]) * (t > 0).astype(F32)], axis=0).astype(BF16)
        o_ref[...] = x
        up(u_stage[0])

    for p in (0, 1):
        @pl.when((j > 0) & (j < nh) & (j % 2 == p))
        def _(p=p):
            up(u_stage[p])
            down(u_stage[1 - p])

    @pl.when(j == nh)
    def _():
        down(u_stage[(nh - 1) % 2])
        if final_norm:
            y = o_ref[...]
            ms = jnp.mean(y * y, axis=-1, keepdims=True)
            o_ref[...] = y * lax.rsqrt(ms + EPS) * fg_ref[...]


def _ffn(x, norm_g, w_up, dw, dwb, w_down, final_g, *, batch, seq, tm, hb, final_norm):
    n, d = x.shape
    nt = seq // tm
    nh = FFN_HIDDEN // hb
    r = tm // FFN_HALO
    last = n // FFN_HALO - 1
    const = lambda b, t, j: (0, 0)
    up_blk = lambda b, t, j: (0, jnp.minimum(j, nh - 1))
    gate_blk = lambda b, t, j: (0, jnp.maximum(j - 1, 0))
    u_stage = pltpu.VMEM((tm + 2 * FFN_HALO, 2 * hb), F32)
    return pl.pallas_call(
        functools.partial(_ffn_kernel, tm=tm, nt=nt, nh=nh, hb=hb, final_norm=final_norm),
        grid=(batch, nt, nh + 1),
        in_specs=[
            pl.BlockSpec((tm, d), lambda b, t, j: (b * nt + t, 0)),
            pl.BlockSpec((FFN_HALO, d), lambda b, t, j: (jnp.maximum((b * nt + t) * r - 1, 0), 0)),
            pl.BlockSpec((FFN_HALO, d), lambda b, t, j: (jnp.minimum((b * nt + t + 1) * r, last), 0)),
            pl.BlockSpec((1, d), const),
            pl.BlockSpec((d, 2 * hb), up_blk),
            pl.BlockSpec((3, 2 * hb), gate_blk),
            pl.BlockSpec((1, 2 * hb), gate_blk),
            pl.BlockSpec((hb, d), lambda b, t, j: (jnp.maximum(j - 1, 0), 0)),
            pl.BlockSpec((1, d), const),
        ],
        out_specs=pl.BlockSpec((tm, d), lambda b, t, j: (b * nt + t, 0)),
        out_shape=jax.ShapeDtypeStruct((n, d), F32),
        scratch_shapes=[pltpu.VMEM((tm + 2 * FFN_HALO, d), BF16), u_stage, u_stage],
        compiler_params=_params("parallel", "parallel", "arbitrary"),
        name="conv_ffn",
    )(x, x, x, norm_g.reshape(1, d), w_up, dw, dwb.reshape(1, -1), w_down, final_g.reshape(1, d))


def _interleave_halves(a):
    lead = a.shape[:-1]
    nc = FFN_HIDDEN // MXU_DIM
    return jnp.swapaxes(a.reshape(*lead, 2, nc, MXU_DIM), -3, -2).reshape(*lead, 2 * FFN_HIDDEN)


def _permute_w_in(w_in):
    sizes = (BR_W, BR_W, SW_HEADS * HEAD_DIM, SW_KV * HEAD_DIM, SW_KV * HEAD_DIM, BR_W, BR_W,
             NA_HEADS * HEAD_DIM, NA_HEADS * HEAD_DIM, NA_HEADS * HEAD_DIM, N_BRANCH * D_MODEL)
    offs = np.concatenate([[0], np.cumsum(sizes)])
    (gm_u, gm_v, sw_q, sw_k, sw_v, cv_a, cv_g, na_q, na_k, na_v, gates) = [
        w_in[:, :, offs[i]:offs[i + 1]] for i in range(len(sizes))]
    scale = HEAD_DIM ** -0.5
    parts = [gates, gm_u, gm_v, sw_q * scale, cv_a, cv_g, na_q * scale, na_k, na_v, sw_k, sw_v]
    return jnp.concatenate(parts, axis=-1).astype(BF16)


def _trunk(x, p, *, tm=512):
    batch, seq, d = x.shape
    n = batch * seq
    x2 = x.reshape(n, d)
    for l in range(DEPTH):
        z = _norm_proj(x2, p["norm1_g"][l], p["w_in"][l], tm=2 * tm, tn=1792)
        o_a = _gmlp(z, p["gm_ln_g"][l], p["gm_ln_b"][l], p["gm_ws"][l], p["gm_bs"][l], tm=tm)
        o_b = _swa(z, p["sw_sink"][l], batch=batch, seq=seq, tq=tm)
        o_c = _convmod(z, p["cv_dw"][l], p["cv_dwb"][l], p["cv_ln_g"][l], p["cv_ln_b"][l],
                       batch=batch, seq=seq, tm=tm)
        o_d = _na(z, p["na_table"][l], batch=batch, seq=seq, rb=16)
        x2 = _merge_proj(z, (o_a, o_b, o_c, o_d), p["gate_b"][l], p["w_branch"][l], p["w_out"][l], x2, tm=tm)
        x2 = _ffn(x2, p["norm2_g"][l], p["w_up"][l], p["ffn_dw"][l], p["ffn_dwb"][l], p["w_down"][l],
                  p["final_g"], batch=batch, seq=seq, tm=tm, hb=512, final_norm=(l == DEPTH - 1))
    return x2.reshape(batch, seq, d)


def kernel(x_prompt, x_sample, norm1_g, w_in, gate_b, gm_ln_g, gm_ln_b, gm_ws, gm_bs, sw_sink, cv_dw, cv_dwb,
           cv_ln_g, cv_ln_b, na_rpb, w_branch, w_out, norm2_g, w_up, ffn_dw, ffn_dwb, w_down, final_g):
    p = dict(norm1_g=norm1_g, w_in=_permute_w_in(w_in), gate_b=gate_b, gm_ln_g=gm_ln_g, gm_ln_b=gm_ln_b,
             gm_ws=gm_ws, gm_bs=gm_bs, sw_sink=sw_sink, cv_dw=cv_dw, cv_dwb=cv_dwb, cv_ln_g=cv_ln_g,
             cv_ln_b=cv_ln_b, na_table=[_na_bias_table(na_rpb[l]) for l in range(DEPTH)],
             w_branch=w_branch.astype(BF16), w_out=w_out.astype(BF16),
             norm2_g=norm2_g, w_up=_interleave_halves(w_up).astype(BF16), ffn_dw=_interleave_halves(ffn_dw),
             ffn_dwb=_interleave_halves(ffn_dwb),
             w_down=w_down.astype(BF16), final_g=final_g)
    return (_trunk(x_prompt, p), _trunk(x_sample, p))
```

```python
import functools

import numpy as np
import jax
import jax.numpy as jnp
from jax import lax
from jax.experimental import pallas as pl
from jax.experimental.pallas import tpu as pltpu

F32 = jnp.float32
BF16 = jnp.bfloat16

D_MODEL = 2048
DEPTH = 2
GRID_W = 64
N_BRANCH = 4
BR_W = 512
HEAD_DIM = 64
CHUNK = 128
GM_GROUPS = 4
SW_HEADS = 8
SW_KV = 2
SW_WIN = 128
ROPE_THETA = 10000.0
CONV_W = 31
NA_HEADS = 8
NA_ROWS = 8
NA_COLS = 16
NA_QROWS = 4
NA_KROWS = 12
NA_T_LOW = 2 * NA_ROWS - 2
NA_T_HIGH = NA_T_LOW + NA_ROWS
NA_T_NONE = NA_T_HIGH + NA_ROWS
FFN_HIDDEN = 5632
EPS = 1e-6
NEG_INF = -1e30

LANES = 128
SUBLANES = 8
MXU_DIM = 256
CONV_HALO = 16
FFN_HALO = SUBLANES

Z_GATES = 0
Z_GM_U = 8192
Z_GM_V = 8704
Z_SW_Q = 9216
Z_CV_A = 9728
Z_CV_G = 10240
Z_NA_Q = 10752
Z_NA_K = 11264
Z_NA_V = 11776
Z_SW_K = 12288
Z_SW_V = 12416
Z_COLS = 12544

VMEM_LIMIT = 56 * 1024 * 1024


def _params(*sem, flags=None):
    return pltpu.CompilerParams(dimension_semantics=sem, vmem_limit_bytes=VMEM_LIMIT, flags=flags)


def _norm_proj_kernel(x_ref, g_ref, w_ref, o_ref, h_ref):
    @pl.when(pl.program_id(1) == 0)
    def _():
        x = x_ref[...]
        ms = jnp.mean(x * x, axis=-1, keepdims=True)
        h_ref[...] = (x * lax.rsqrt(ms + EPS) * g_ref[...]).astype(BF16)

    o_ref[...] = jnp.dot(h_ref[...], w_ref[...], preferred_element_type=F32).astype(o_ref.dtype)


def _norm_proj(x, g, w, *, tm, tn):
    n, d = x.shape
    nout = w.shape[1]
    return pl.pallas_call(
        _norm_proj_kernel,
        grid=(n // tm, nout // tn),
        in_specs=[
            pl.BlockSpec((tm, d), lambda i, j: (i, 0)),
            pl.BlockSpec((1, d), lambda i, j: (0, 0)),
            pl.BlockSpec((d, tn), lambda i, j: (0, j)),
        ],
        out_specs=pl.BlockSpec((tm, tn), lambda i, j: (i, j)),
        out_shape=jax.ShapeDtypeStruct((n, nout), BF16),
        scratch_shapes=[pltpu.VMEM((tm, d), BF16)],
        compiler_params=_params("parallel", "arbitrary"),
        name="norm_proj",
    )(x, g.reshape(1, d), w)


def _layernorm_rows(x, g, b):
    mu = jnp.mean(x, axis=-1, keepdims=True)
    xc = x - mu
    var = jnp.mean(xc * xc, axis=-1, keepdims=True)
    return xc * lax.rsqrt(var + EPS) * g + b


def _gmlp_kernel(u_ref, v_ref, lng_ref, lnb_ref, ws_ref, bs_ref, o_ref, *, tm):
    vn = _layernorm_rows(v_ref[...].astype(F32), lng_ref[...], lnb_ref[...]).astype(BF16)
    gw = BR_W // GM_GROUPS
    for c in range(tm // CHUNK):
        rows = slice(c * CHUNK, (c + 1) * CHUNK)
        for g in range(GM_GROUPS):
            cols = slice(g * gw, (g + 1) * gw)
            mixed = jnp.dot(ws_ref[g], vn[rows, cols], preferred_element_type=F32) + bs_ref[:, cols]
            o_ref[rows, cols] = (u_ref[rows, cols].astype(F32) * mixed).astype(o_ref.dtype)


def _gmlp(z, ln_g, ln_b, ws, bs, *, tm):
    n = z.shape[0]
    gw = BR_W // GM_GROUPS
    bs_full = jnp.repeat(bs.T, gw, axis=1)
    const = lambda i: (0, 0)
    return pl.pallas_call(
        functools.partial(_gmlp_kernel, tm=tm),
        grid=(n // tm,),
        in_specs=[
            pl.BlockSpec((tm, BR_W), lambda i: (i, Z_GM_U // BR_W)),
            pl.BlockSpec((tm, BR_W), lambda i: (i, Z_GM_V // BR_W)),
            pl.BlockSpec((1, BR_W), const),
            pl.BlockSpec((1, BR_W), const),
            pl.BlockSpec((GM_GROUPS, CHUNK, CHUNK), lambda i: (0, 0, 0)),
            pl.BlockSpec((CHUNK, BR_W), const),
        ],
        out_specs=pl.BlockSpec((tm, BR_W), lambda i: (i, 0)),
        out_shape=jax.ShapeDtypeStruct((n, BR_W), BF16),
        compiler_params=_params("parallel"),
        name="gmlp",
    )(z, z, ln_g.reshape(1, BR_W), ln_b.reshape(1, BR_W), ws.astype(BF16), bs_full)


def _rope(x, cos, sin, first_half):
    w = x.shape[1]
    reps = w // LANES
    if reps > 1:
        cos = jnp.concatenate([cos] * reps, axis=1)
        sin = jnp.concatenate([sin] * reps, axis=1)
        first_half = jnp.concatenate([first_half] * reps, axis=1)
    half = HEAD_DIM // 2
    lower = pltpu.roll(x, half, 1)
    upper = pltpu.roll(x, w - half, 1)
    return x * cos + jnp.where(first_half, -upper, lower) * sin


def _swa_kernel(sink_ref, q_ref, kp_ref, k_ref, kn_ref, vp_ref, v_ref, vn_ref, cos_ref, sin_ref,
                o_ref, qs_ref, kd_ref, vd_ref, *, tq, seq):
    t = pl.program_id(1)
    base = t * tq
    blk = SW_WIN
    lane = lax.broadcasted_iota(jnp.int32, (1, LANES), 1)
    first_half = (lane % HEAD_DIM) < (HEAD_DIM // 2)
    lane_lo = lane < HEAD_DIM

    def table(ref, start, size):
        return ref[pl.ds(pl.multiple_of(start, blk), size), :]

    p_start = jnp.maximum(base - blk, 0)
    n_start = jnp.minimum(base + tq, seq - blk)
    cq, sq = table(cos_ref, base, tq), table(sin_ref, base, tq)
    qs_ref[...] = _rope(q_ref[...].astype(F32), cq, sq, first_half).astype(BF16)
    k_ext = jnp.concatenate([
        _rope(kp_ref[...].astype(F32), table(cos_ref, p_start, blk), table(sin_ref, p_start, blk), first_half),
        _rope(k_ref[...].astype(F32), cq, sq, first_half),
        _rope(kn_ref[...].astype(F32), table(cos_ref, n_start, blk), table(sin_ref, n_start, blk), first_half),
    ], axis=0)
    v_ext = jnp.concatenate([vp_ref[...], v_ref[...], vn_ref[...]], axis=0).astype(F32)
    for src, dst in ((k_ext, kd_ref), (v_ext, vd_ref)):
        swapped = pltpu.roll(src, HEAD_DIM, 1)
        dst[0] = jnp.where(lane_lo, src, swapped).astype(BF16)
        dst[1] = jnp.where(lane_lo, swapped, src).astype(BF16)

    g = SW_HEADS // SW_KV
    qi_idx = lax.broadcasted_iota(jnp.int32, (blk, 3 * blk), 0)
    ki_idx = lax.broadcasted_iota(jnp.int32, (blk, 3 * blk), 1)
    band = (ki_idx >= qi_idx) & (ki_idx - qi_idx <= 2 * SW_WIN)
    for i in range(tq // blk):
        kpos = base + (i - 1) * blk + ki_idx
        valid = band & (kpos >= 0) & (kpos < seq)
        qi = qs_ref[i * blk:(i + 1) * blk, :]
        outs = []
        for j in range(SW_KV):
            kj = kd_ref[j, i * blk:(i + 3) * blk, :]
            vj = vd_ref[j, i * blk:(i + 3) * blk, :]
            parts = []
            for gi in range(g):
                h = g * j + gi
                qp = qi[:, (h // 2) * LANES:(h // 2 + 1) * LANES]
                keep = lane_lo if h % 2 == 0 else jnp.logical_not(lane_lo)
                parts.append(jnp.where(keep, qp, jnp.zeros_like(qp)))
            s_all = lax.dot_general(jnp.concatenate(parts, axis=0), kj, (((1,), (1,)), ((), ())),
                                    preferred_element_type=F32)
            ps, rden = [], []
            for gi in range(g):
                sink = sink_ref[g * j + gi]
                s = jnp.where(valid, s_all[gi * blk:(gi + 1) * blk], NEG_INF)
                m = jnp.maximum(jnp.max(s, axis=-1, keepdims=True), sink)
                p = jnp.exp(s - m)
                rden.append(1.0 / (jnp.sum(p, axis=-1, keepdims=True) + jnp.exp(sink - m)))
                ps.append(p.astype(BF16))
            o = jnp.dot(jnp.concatenate(ps, axis=0), vj, preferred_element_type=F32)
            outs.append(o * jnp.concatenate(rden, axis=0))
        for pr in range(SW_HEADS // 2):
            j, g0 = (2 * pr) // g, (2 * pr) % g
            pair = jnp.where(lane_lo, outs[j][g0 * blk:(g0 + 1) * blk], outs[j][(g0 + 1) * blk:(g0 + 2) * blk])
            o_ref[i * blk:(i + 1) * blk, pr * LANES:(pr + 1) * LANES] = pair.astype(o_ref.dtype)


def _rope_tables(seq):
    half = HEAD_DIM // 2
    inv = jnp.power(ROPE_THETA, -jnp.arange(half, dtype=F32) / half)
    ang = jnp.arange(seq, dtype=F32)[:, None] * inv[None, :]
    reps = LANES // half
    return jnp.tile(jnp.cos(ang), (1, reps)), jnp.tile(jnp.sin(ang), (1, reps))


def _swa(z, sink, *, batch, seq, tq):
    n = z.shape[0]
    blk = SW_WIN
    nt = seq // tq
    r = tq // blk
    cos_t, sin_t = _rope_tables(seq)
    kw = SW_KV * HEAD_DIM
    kcol, vcol = Z_SW_K // kw, Z_SW_V // kw
    nblk = n // blk

    def main(col):
        return lambda b, t: (b * nt + t, col)

    def prev(col):
        return lambda b, t: (jnp.maximum((b * nt + t) * r - 1, 0), col)

    def nxt(col):
        return lambda b, t: (jnp.minimum((b * nt + t + 1) * r, nblk - 1), col)

    const = lambda b, t: (0, 0)
    return pl.pallas_call(
        functools.partial(_swa_kernel, tq=tq, seq=seq),
        grid=(batch, nt),
        in_specs=[
            pl.BlockSpec(memory_space=pltpu.SMEM),
            pl.BlockSpec((tq, BR_W), main(Z_SW_Q // BR_W)),
            pl.BlockSpec((blk, kw), prev(kcol)),
            pl.BlockSpec((tq, kw), main(kcol)),
            pl.BlockSpec((blk, kw), nxt(kcol)),
            pl.BlockSpec((blk, kw), prev(vcol)),
            pl.BlockSpec((tq, kw), main(vcol)),
            pl.BlockSpec((blk, kw), nxt(vcol)),
            pl.BlockSpec((seq, LANES), const),
            pl.BlockSpec((seq, LANES), const),
        ],
        out_specs=pl.BlockSpec((tq, BR_W), lambda b, t: (b * nt + t, 0)),
        out_shape=jax.ShapeDtypeStruct((n, BR_W), BF16),
        scratch_shapes=[
            pltpu.VMEM((tq, BR_W), BF16),
            pltpu.VMEM((SW_KV, tq + 2 * blk, LANES), BF16),
            pltpu.VMEM((SW_KV, tq + 2 * blk, LANES), BF16),
        ],
        compiler_params=_params("parallel", "parallel"),
        name="swa",
    )(sink, z, z, z, z, z, z, z, cos_t, sin_t)


def _convmod_kernel(a_ref, g_ref, ap_ref, gp_ref, an_ref, gn_ref, dw_ref, dwb_ref, lng_ref, lnb_ref,
                    o_ref, xs_ref, *, tm, nt, rc):
    t = pl.program_id(1)
    hal = CONV_HALO
    rows = tm + 2 * hal

    def glu(a, g):
        return a[...].astype(F32) * jax.nn.sigmoid(g[...].astype(F32))

    xs_ref[0, 0:hal, :] = glu(ap_ref, gp_ref) * (t > 0).astype(F32)
    xs_ref[0, hal:hal + tm, :] = glu(a_ref, g_ref)
    xs_ref[0, hal + tm:rows, :] = glu(an_ref, gn_ref) * (t < nt - 1).astype(F32)
    for r0 in range(0, rows - SUBLANES, rc):
        nr = min(rc, rows - SUBLANES - r0)
        x = xs_ref[0, r0:r0 + nr + SUBLANES, :]
        for s in range(1, SUBLANES):
            xs_ref[s, r0:r0 + nr, :] = pltpu.roll(x, nr + SUBLANES - s, 0)[0:nr]
    pad = CONV_W // 2
    for c in range(tm // rc):
        r0 = c * rc
        acc = jnp.zeros((rc, BR_W), F32)
        for k in range(CONV_W):
            off = hal - pad + k
            al = r0 + (off // SUBLANES) * SUBLANES
            acc = acc + dw_ref[k:k + 1, :] * xs_ref[off % SUBLANES, al:al + rc, :]
        y = _layernorm_rows(acc + dwb_ref[...], lng_ref[...], lnb_ref[...])
        o_ref[r0:r0 + rc, :] = (y * jax.nn.sigmoid(y)).astype(o_ref.dtype)


def _halo_specs(width, col, *, nt, tm, halo, nrows):
    r = tm // halo
    last = nrows // halo - 1
    return (
        pl.BlockSpec((tm, width), lambda b, t: (b * nt + t, col)),
        pl.BlockSpec((halo, width), lambda b, t: (jnp.maximum((b * nt + t) * r - 1, 0), col)),
        pl.BlockSpec((halo, width), lambda b, t: (jnp.minimum((b * nt + t + 1) * r, last), col)),
    )


def _convmod(z, dw, dwb, ln_g, ln_b, *, batch, seq, tm):
    n = z.shape[0]
    nt = seq // tm
    a_m, a_p, a_n = _halo_specs(BR_W, Z_CV_A // BR_W, nt=nt, tm=tm, halo=CONV_HALO, nrows=n)
    g_m, g_p, g_n = _halo_specs(BR_W, Z_CV_G // BR_W, nt=nt, tm=tm, halo=CONV_HALO, nrows=n)
    const = lambda b, t: (0, 0)
    row = lambda v: v.reshape(1, BR_W)
    return pl.pallas_call(
        functools.partial(_convmod_kernel, tm=tm, nt=nt, rc=64),
        grid=(batch, nt),
        in_specs=[a_m, g_m, a_p, g_p, a_n, g_n,
                  pl.BlockSpec((CONV_W, BR_W), const),
                  pl.BlockSpec((1, BR_W), const),
                  pl.BlockSpec((1, BR_W), const),
                  pl.BlockSpec((1, BR_W), const)],
        out_specs=pl.BlockSpec((tm, BR_W), lambda b, t: (b * nt + t, 0)),
        out_shape=jax.ShapeDtypeStruct((n, BR_W), BF16),
        scratch_shapes=[pltpu.VMEM((SUBLANES, tm + 2 * CONV_HALO, BR_W), F32)],
        compiler_params=_params("parallel", "parallel"),
        name="convmod",
    )(z, z, z, z, z, z, dw, row(dwb), row(ln_g), row(ln_b))


def _na_kernel(q_ref, k_ref, v_ref, bias_ref, o_ref, *, rb, rows):
    lane = lax.broadcasted_iota(jnp.int32, (1, LANES), 1)
    lane_lo = lane < HEAD_DIM
    row0 = pl.program_id(1) * rb
    npair = NA_KROWS // 2

    def one_group(gi, carry):
        r0 = row0 + gi * NA_QROWS
        ks = jnp.clip(r0 - NA_ROWS // 2, 0, rows - NA_KROWS)
        q_rows = pl.ds(pl.multiple_of(gi * (NA_QROWS * GRID_W), NA_QROWS * GRID_W), NA_QROWS * GRID_W)
        k_rows = pl.ds(pl.multiple_of(ks * GRID_W, GRID_W), NA_KROWS * GRID_W)
        entry = []
        for qi in range(NA_QROWS):
            r = r0 + qi
            rs = jnp.clip(r - NA_ROWS // 2, 0, rows - NA_ROWS)
            for w in range(npair):
                kr = ks + 2 * w
                d = kr - r + NA_ROWS - 1
                in0 = (kr >= rs) & (kr < rs + NA_ROWS)
                in1 = (kr + 1 >= rs) & (kr + 1 < rs + NA_ROWS)
                entry.append(jnp.where(in0 & in1, d,
                                       jnp.where(in1, NA_T_LOW + d + 1,
                                                 jnp.where(in0, NA_T_HIGH + d - (NA_ROWS - 1), NA_T_NONE))))
        for pr in range(NA_HEADS // 2):
            cols = slice(pr * LANES, (pr + 1) * LANES)
            qp = q_ref[q_rows, cols]
            kp = k_ref[k_rows, cols]
            vp = v_ref[k_rows, cols]
            halves = []
            for hh in range(2):
                h = 2 * pr + hh
                keep = lane_lo if hh == 0 else jnp.logical_not(lane_lo)
                qm = jnp.where(keep, qp, jnp.zeros_like(qp))
                s = lax.dot_general(qm, kp, (((1,), (1,)), ((), ())), preferred_element_type=F32)
                bias = jnp.concatenate([
                    jnp.concatenate([bias_ref[h, entry[qi * npair + w]] for w in range(npair)], axis=1)
                    for qi in range(NA_QROWS)], axis=0)
                s = s + bias
                m = jnp.max(s, axis=-1, keepdims=True)
                p = jnp.exp(s - m)
                rden = 1.0 / jnp.sum(p, axis=-1, keepdims=True)
                halves.append(jnp.dot(p.astype(BF16), vp, preferred_element_type=F32) * rden)
            o_ref[q_rows, cols] = jnp.where(lane_lo, halves[0], halves[1]).astype(o_ref.dtype)
        return carry

    lax.fori_loop(0, rb // NA_QROWS, one_group, 0)


def _na_bias_table(rpb):
    c = np.arange(GRID_W)
    cs = np.clip(c - NA_COLS // 2, 0, GRID_W - NA_COLS)
    kc = np.arange(GRID_W)
    ok = (kc[None, :] >= cs[:, None]) & (kc[None, :] < cs[:, None] + NA_COLS)
    dc = np.clip(kc[None, :] - c[:, None], 1 - NA_COLS, NA_COLS - 1) + NA_COLS - 1
    full = jnp.where(ok[None, None], rpb.astype(F32)[:, :, dc], NEG_INF)
    out = jnp.full_like(full[:, :NA_ROWS], NEG_INF)
    return jnp.concatenate([
        jnp.concatenate([full[:, :-1], full[:, 1:]], axis=-1),
        jnp.concatenate([out, full[:, :NA_ROWS]], axis=-1),
        jnp.concatenate([full[:, NA_ROWS - 1:], out], axis=-1),
        jnp.concatenate([out[:, :1], out[:, :1]], axis=-1)], axis=1)


def _na(z, table, *, batch, seq, rb):
    n = z.shape[0]
    rows = seq // GRID_W
    nrb = rows // rb
    return pl.pallas_call(
        functools.partial(_na_kernel, rb=rb, rows=rows),
        grid=(batch, nrb),
        in_specs=[
            pl.BlockSpec((rb * GRID_W, BR_W), lambda b, t: (b * nrb + t, Z_NA_Q // BR_W)),
            pl.BlockSpec((seq, BR_W), lambda b, t: (b, Z_NA_K // BR_W)),
            pl.BlockSpec((seq, BR_W), lambda b, t: (b, Z_NA_V // BR_W)),
            pl.BlockSpec(table.shape, lambda b, t: (0, 0, 0, 0), pipeline_mode=pl.Buffered(1)),
        ],
        out_specs=pl.BlockSpec((rb * GRID_W, BR_W), lambda b, t: (b * nrb + t, 0)),
        out_shape=jax.ShapeDtypeStruct((n, BR_W), BF16),
        compiler_params=_params("parallel", "arbitrary"),
        name="nattn",
    )(z, z, z, table)


def _merge_proj_kernel(oa_ref, ob_ref, oc_ref, od_ref, g_ref, bias_ref, wb_ref, wo_ref, x_ref, o_ref):
    k = pl.program_id(1)
    tm = o_ref.shape[0]
    for kk, br_ref in enumerate((oa_ref, ob_ref, oc_ref, od_ref)):
        @pl.when(k == kk)
        def _(kk=kk, br_ref=br_ref):
            for r0 in range(0, tm, MXU_DIM):
                rows = slice(r0, r0 + MXU_DIM)
                y = jnp.dot(br_ref[rows, :], wb_ref[kk], preferred_element_type=F32)
                th = jnp.tanh(0.5 * (g_ref[rows, :].astype(F32) + bias_ref[kk:kk + 1, :]))
                term = (0.5 * y) * (th + 1.0)
                if kk == 0:
                    o_ref[rows, :] = term
                elif kk < N_BRANCH - 1:
                    o_ref[rows, :] += term
                else:
                    merged = (o_ref[rows, :] + term).astype(BF16)
                    o_ref[rows, :] = x_ref[rows, :] + jnp.dot(merged, wo_ref[...], preferred_element_type=F32)


def _merge_proj(z, branches, gate_b, w_branch, w_out, x, *, tm):
    n, d = x.shape
    br_spec = pl.BlockSpec((tm, BR_W), lambda i, k: (i, 0))
    return pl.pallas_call(
        _merge_proj_kernel,
        grid=(n // tm, N_BRANCH),
        in_specs=[br_spec] * N_BRANCH + [
            pl.BlockSpec((tm, d), lambda i, k: (i, Z_GATES // d + k)),
            pl.BlockSpec((N_BRANCH, d), lambda i, k: (0, 0)),
            pl.BlockSpec((N_BRANCH, BR_W, d), lambda i, k: (0, 0, 0), pipeline_mode=pl.Buffered(1)),
            pl.BlockSpec((d, d), lambda i, k: (0, 0), pipeline_mode=pl.Buffered(1)),
            pl.BlockSpec((tm, d), lambda i, k: (i, 0)),
        ],
        out_specs=pl.BlockSpec((tm, d), lambda i, k: (i, 0)),
        out_shape=jax.ShapeDtypeStruct((n, d), F32),
        compiler_params=_params("parallel", "arbitrary"),
        name="merge_proj",
    )(*branches, z, gate_b, w_branch, w_out, x)


def _ffn_kernel(x_ref, xp_ref, xn_ref, g_ref, wa_ref, wb_ref, dwa_ref, dwb_ref, ba_ref, bb_ref, wd_ref, fg_ref,
                o_ref, h_ref, u0_ref, u1_ref, *, tm, nt, nh, hb, final_norm):
    t = pl.program_id(1)
    j = pl.program_id(2)
    rows = tm + 2 * FFN_HALO
    u_stage = (u0_ref, u1_ref)

    def norm(x):
        ms = jnp.mean(x * x, axis=-1, keepdims=True)
        return x * lax.rsqrt(ms + EPS) * g_ref[...]

    def up(dst_ref):
        h = h_ref[...]
        dst_ref[:, 0:hb] = jnp.dot(h, wa_ref[...], preferred_element_type=F32)
        dst_ref[:, hb:2 * hb] = jnp.dot(h, wb_ref[...], preferred_element_type=F32)

    def down(src_ref):
        part = None
        for c in range(hb // MXU_DIM):
            ca = slice(c * MXU_DIM, (c + 1) * MXU_DIM)
            cb = slice(hb + c * MXU_DIM, hb + (c + 1) * MXU_DIM)
            pair = lambda ra, rb, rows_: jnp.concatenate([ra[rows_, ca], rb[rows_, ca]], axis=1)
            u = jnp.concatenate([src_ref[:, ca], src_ref[:, cb]], axis=1)
            below = pltpu.roll(u, 1, 0)[0:tm]
            above = pltpu.roll(u, rows - 1, 0)[0:tm]
            ab = (pair(dwa_ref, dwb_ref, slice(0, 1)) * below + pair(dwa_ref, dwb_ref, slice(1, 2)) * u[0:tm]
                  + pair(dwa_ref, dwb_ref, slice(2, 3)) * above + pair(ba_ref, bb_ref, slice(0, 1)))
            a, b = ab[:, :MXU_DIM], ab[:, MXU_DIM:]
            gated = ((a * b) * (0.5 * jnp.tanh(0.5 * a) + 0.5)).astype(BF16)
            pc = jnp.dot(gated, wd_ref[c * MXU_DIM:(c + 1) * MXU_DIM, :], preferred_element_type=F32)
            part = pc if part is None else part + pc
        o_ref[...] += part

    @pl.when(j == 0)
    def _():
        x = x_ref[...]
        h_ref[0:tm, :] = norm(x).astype(BF16)
        h_ref[tm:rows, :] = jnp.concatenate([
            norm(xn_ref[...]) * (t < nt - 1).astype(F32),
            norm(xp_ref[...]) * (t > 0).astype(F32)], axis=0).astype(BF16)
        o_ref[...] = x
        up(u_stage[0])

    for p in (0, 1):
        @pl.when((j > 0) & (j < nh) & (j % 2 == p))
        def _(p=p):
            up(u_stage[p])
            down(u_stage[1 - p])

    @pl.when(j == nh)
    def _():
        down(u_stage[(nh - 1) % 2])
        if final_norm:
            y = o_ref[...]
            ms = jnp.mean(y * y, axis=-1, keepdims=True)
            o_ref[...] = y * lax.rsqrt(ms + EPS) * fg_ref[...]


def _ffn(x, norm_g, w_up, dw, dwb, w_down, final_g, *, batch, seq, tm, hb, final_norm, single_buffer_x):
    n, d = x.shape
    nt = seq // tm
    nh = FFN_HIDDEN // hb
    r = tm // FFN_HALO
    last = n // FFN_HALO - 1
    const = lambda b, t, j: (0, 0)
    up_a = lambda b, t, j: (0, jnp.minimum(j, nh - 1))
    up_b = lambda b, t, j: (0, nh + jnp.minimum(j, nh - 1))
    gate_a = lambda b, t, j: (0, jnp.maximum(j - 1, 0))
    gate_b = lambda b, t, j: (0, nh + jnp.maximum(j - 1, 0))
    u_stage = pltpu.VMEM((tm + 2 * FFN_HALO, 2 * hb), F32)
    dwb = dwb.reshape(1, -1)
    x_mode = dict(pipeline_mode=pl.Buffered(1)) if single_buffer_x else {}
    return pl.pallas_call(
        functools.partial(_ffn_kernel, tm=tm, nt=nt, nh=nh, hb=hb, final_norm=final_norm),
        grid=(batch, nt, nh + 1),
        in_specs=[
            pl.BlockSpec((tm, d), lambda b, t, j: (b * nt + t, 0), **x_mode),
            pl.BlockSpec((FFN_HALO, d), lambda b, t, j: (jnp.maximum((b * nt + t) * r - 1, 0), 0)),
            pl.BlockSpec((FFN_HALO, d), lambda b, t, j: (jnp.minimum((b * nt + t + 1) * r, last), 0)),
            pl.BlockSpec((1, d), const),
            pl.BlockSpec((d, hb), up_a),
            pl.BlockSpec((d, hb), up_b),
            pl.BlockSpec((3, hb), gate_a),
            pl.BlockSpec((3, hb), gate_b),
            pl.BlockSpec((1, hb), gate_a),
            pl.BlockSpec((1, hb), gate_b),
            pl.BlockSpec((hb, d), lambda b, t, j: (jnp.maximum(j - 1, 0), 0)),
            pl.BlockSpec((1, d), const),
        ],
        out_specs=pl.BlockSpec((tm, d), lambda b, t, j: (b * nt + t, 0)),
        out_shape=jax.ShapeDtypeStruct((n, d), F32),
        scratch_shapes=[pltpu.VMEM((tm + 2 * FFN_HALO, d), BF16), u_stage, u_stage],
        compiler_params=_params("parallel", "parallel", "arbitrary"),
        name="conv_ffn",
    )(x, x, x, norm_g.reshape(1, d), w_up, w_up, dw, dw, dwb, dwb, w_down, final_g.reshape(1, d))


def _permute_w_in(w_in):
    sizes = (BR_W, BR_W, SW_HEADS * HEAD_DIM, SW_KV * HEAD_DIM, SW_KV * HEAD_DIM, BR_W, BR_W,
             NA_HEADS * HEAD_DIM, NA_HEADS * HEAD_DIM, NA_HEADS * HEAD_DIM, N_BRANCH * D_MODEL)
    offs = np.concatenate([[0], np.cumsum(sizes)])
    (gm_u, gm_v, sw_q, sw_k, sw_v, cv_a, cv_g, na_q, na_k, na_v, gates) = [
        w_in[:, :, offs[i]:offs[i + 1]] for i in range(len(sizes))]
    scale = HEAD_DIM ** -0.5
    parts = [gates, gm_u, gm_v, sw_q * scale, cv_a, cv_g, na_q * scale, na_k, na_v, sw_k, sw_v]
    return jnp.concatenate(parts, axis=-1).astype(BF16)


def _trunk(x, p, *, tm=512):
    batch, seq, d = x.shape
    n = batch * seq
    x2 = x.reshape(n, d)
    for l in range(DEPTH):
        z = _norm_proj(x2, p["norm1_g"][l], p["w_in"][l], tm=2 * tm, tn=1792)
        o_a = _gmlp(z, p["gm_ln_g"][l], p["gm_ln_b"][l], p["gm_ws"][l], p["gm_bs"][l], tm=tm)
        o_b = _swa(z, p["sw_sink"][l], batch=batch, seq=seq, tq=tm)
        o_c = _convmod(z, p["cv_dw"][l], p["cv_dwb"][l], p["cv_ln_g"][l], p["cv_ln_b"][l],
                       batch=batch, seq=seq, tm=tm)
        o_d = _na(z, p["na_table"][l], batch=batch, seq=seq, rb=16)
        x2 = _merge_proj(z, (o_a, o_b, o_c, o_d), p["gate_b"][l], p["w_branch"][l], p["w_out"][l], x2, tm=tm)
        x2 = _ffn(x2, p["norm2_g"][l], p["w_up"][l], p["ffn_dw"][l], p["ffn_dwb"][l], p["w_down"][l],
                  p["final_g"], batch=batch, seq=seq, tm=2 * tm, hb=512, final_norm=(l == DEPTH - 1),
                  single_buffer_x=True)
    return x2.reshape(batch, seq, d)


def kernel(x_prompt, x_sample, norm1_g, w_in, gate_b, gm_ln_g, gm_ln_b, gm_ws, gm_bs, sw_sink, cv_dw, cv_dwb,
           cv_ln_g, cv_ln_b, na_rpb, w_branch, w_out, norm2_g, w_up, ffn_dw, ffn_dwb, w_down, final_g):
    p = dict(norm1_g=norm1_g, w_in=_permute_w_in(w_in), gate_b=gate_b, gm_ln_g=gm_ln_g, gm_ln_b=gm_ln_b,
             gm_ws=gm_ws, gm_bs=gm_bs, sw_sink=sw_sink, cv_dw=cv_dw, cv_dwb=cv_dwb, cv_ln_g=cv_ln_g,
             cv_ln_b=cv_ln_b, na_table=[_na_bias_table(na_rpb[l]) for l in range(DEPTH)],
             w_branch=w_branch.astype(BF16), w_out=w_out.astype(BF16),
             norm2_g=norm2_g, w_up=w_up.astype(BF16), ffn_dw=ffn_dw, ffn_dwb=ffn_dwb,
             w_down=w_down.astype(BF16), final_g=final_g)
    return (_trunk(x_prompt, p), _trunk(x_sample, p))
```

```python
import functools

import numpy as np
import jax
import jax.numpy as jnp
from jax import lax
from jax.experimental import pallas as pl
from jax.experimental.pallas import tpu as pltpu

F32 = jnp.float32
BF16 = jnp.bfloat16

D_MODEL = 2048
DEPTH = 2
GRID_W = 64
N_BRANCH = 4
BR_W = 512
HEAD_DIM = 64
CHUNK = 128
GM_GROUPS = 4
SW_HEADS = 8
SW_KV = 2
SW_WIN = 128
ROPE_THETA = 10000.0
CONV_W = 31
NA_HEADS = 8
NA_ROWS = 8
NA_COLS = 16
NA_QROWS = 4
NA_KROWS = 12
NA_T_LOW = 2 * NA_ROWS - 2
NA_T_HIGH = NA_T_LOW + NA_ROWS
NA_T_NONE = NA_T_HIGH + NA_ROWS
FFN_HIDDEN = 5632
EPS = 1e-6
NEG_INF = -1e30

LANES = 128
SUBLANES = 8
MXU_DIM = 256
CONV_HALO = 16
FFN_HALO = SUBLANES
FFN_ROW_CHUNK = 256

Z_GATES = 0
Z_GM_U = 8192
Z_GM_V = 8704
Z_SW_Q = 9216
Z_SW_K = 9728
Z_SW_V = 9856
Z_CV_A = 9984
Z_CV_G = 10496
Z_NA_Q = 11008
Z_NA_K = 11520
Z_NA_V = 12032
Z_COLS = 12544
HALF_W = BR_W // 2

VMEM_LIMIT = 56 * 1024 * 1024


def _params(*sem):
    return pltpu.CompilerParams(dimension_semantics=sem, vmem_limit_bytes=VMEM_LIMIT)


def _norm_proj_kernel(x_ref, g_ref, w_ref, o_ref, h_ref):
    @pl.when(pl.program_id(1) == 0)
    def _():
        x = x_ref[...]
        ms = jnp.mean(x * x, axis=-1, keepdims=True)
        h_ref[...] = (x * lax.rsqrt(ms + EPS) * g_ref[...]).astype(BF16)

    o_ref[...] = jnp.dot(h_ref[...], w_ref[...], preferred_element_type=F32).astype(o_ref.dtype)


def _norm_proj(x, g, w, *, tm, tn):
    n, d = x.shape
    nout = w.shape[1]
    return pl.pallas_call(
        _norm_proj_kernel,
        grid=(n // tm, nout // tn),
        in_specs=[
            pl.BlockSpec((tm, d), lambda i, j: (i, 0)),
            pl.BlockSpec((1, d), lambda i, j: (0, 0)),
            pl.BlockSpec((d, tn), lambda i, j: (0, j)),
        ],
        out_specs=pl.BlockSpec((tm, tn), lambda i, j: (i, j)),
        out_shape=jax.ShapeDtypeStruct((n, nout), BF16),
        scratch_shapes=[pltpu.VMEM((tm, d), BF16)],
        compiler_params=_params("parallel", "arbitrary"),
        name="norm_proj",
    )(x, g.reshape(1, d), w)


def _layernorm_rows(x, g, b):
    mu = jnp.mean(x, axis=-1, keepdims=True)
    xc = x - mu
    var = jnp.mean(xc * xc, axis=-1, keepdims=True)
    return xc * lax.rsqrt(var + EPS) * g + b


def _gmlp_kernel(u_ref, v_ref, lng_ref, lnb_ref, ws_ref, bs_ref, o_ref, *, tm):
    vn = _layernorm_rows(v_ref[...].astype(F32), lng_ref[...], lnb_ref[...]).astype(BF16)
    gw = BR_W // GM_GROUPS
    for c in range(tm // CHUNK):
        rows = slice(c * CHUNK, (c + 1) * CHUNK)
        for g in range(GM_GROUPS):
            cols = slice(g * gw, (g + 1) * gw)
            mixed = jnp.dot(ws_ref[g], vn[rows, cols], preferred_element_type=F32) + bs_ref[:, cols]
            o_ref[rows, cols] = (u_ref[rows, cols].astype(F32) * mixed).astype(o_ref.dtype)


def _gmlp(z, ln_g, ln_b, ws, bs, *, tm):
    n = z.shape[0]
    gw = BR_W // GM_GROUPS
    bs_full = jnp.repeat(bs.T, gw, axis=1)
    const = lambda i: (0, 0)
    return pl.pallas_call(
        functools.partial(_gmlp_kernel, tm=tm),
        grid=(n // tm,),
        in_specs=[
            pl.BlockSpec((tm, BR_W), lambda i: (i, Z_GM_U // BR_W)),
            pl.BlockSpec((tm, BR_W), lambda i: (i, Z_GM_V // BR_W)),
            pl.BlockSpec((1, BR_W), const),
            pl.BlockSpec((1, BR_W), const),
            pl.BlockSpec((GM_GROUPS, CHUNK, CHUNK), lambda i: (0, 0, 0)),
            pl.BlockSpec((CHUNK, BR_W), const),
        ],
        out_specs=pl.BlockSpec((tm, BR_W), lambda i: (i, 0)),
        out_shape=jax.ShapeDtypeStruct((n, BR_W), BF16),
        compiler_params=_params("parallel"),
        name="gmlp",
    )(z, z, ln_g.reshape(1, BR_W), ln_b.reshape(1, BR_W), ws.astype(BF16), bs_full)


def _rope(x, cos, sin, first_half):
    w = x.shape[1]
    reps = w // LANES
    if reps > 1:
        cos = jnp.concatenate([cos] * reps, axis=1)
        sin = jnp.concatenate([sin] * reps, axis=1)
        first_half = jnp.concatenate([first_half] * reps, axis=1)
    half = HEAD_DIM // 2
    lower = pltpu.roll(x, half, 1)
    upper = pltpu.roll(x, w - half, 1)
    return x * cos + jnp.where(first_half, -upper, lower) * sin


def _swa_kernel(sink_ref, q_ref, kp_ref, k_ref, kn_ref, vp_ref, v_ref, vn_ref, cos_ref, sin_ref,
                o_ref, qs_ref, kd_ref, vd_ref, *, tq, seq):
    t = pl.program_id(1)
    base = t * tq
    blk = SW_WIN
    lane = lax.broadcasted_iota(jnp.int32, (1, LANES), 1)
    first_half = (lane % HEAD_DIM) < (HEAD_DIM // 2)
    lane_lo = lane < HEAD_DIM

    def table(ref, start, size):
        return ref[pl.ds(pl.multiple_of(start, blk), size), :]

    p_start = jnp.maximum(base - blk, 0)
    n_start = jnp.minimum(base + tq, seq - blk)
    cq, sq = table(cos_ref, base, tq), table(sin_ref, base, tq)
    qs_ref[...] = _rope(q_ref[...].astype(F32), cq, sq, first_half).astype(BF16)
    k_ext = jnp.concatenate([
        _rope(kp_ref[...].astype(F32), table(cos_ref, p_start, blk), table(sin_ref, p_start, blk), first_half),
        _rope(k_ref[...].astype(F32), cq, sq, first_half),
        _rope(kn_ref[...].astype(F32), table(cos_ref, n_start, blk), table(sin_ref, n_start, blk), first_half),
    ], axis=0)
    v_ext = jnp.concatenate([vp_ref[...], v_ref[...], vn_ref[...]], axis=0).astype(F32)
    for src, dst in ((k_ext, kd_ref), (v_ext, vd_ref)):
        swapped = pltpu.roll(src, HEAD_DIM, 1)
        dst[0] = jnp.where(lane_lo, src, swapped).astype(BF16)
        dst[1] = jnp.where(lane_lo, swapped, src).astype(BF16)

    g = SW_HEADS // SW_KV
    qi_idx = lax.broadcasted_iota(jnp.int32, (blk, 3 * blk), 0)
    ki_idx = lax.broadcasted_iota(jnp.int32, (blk, 3 * blk), 1)
    band = (ki_idx >= qi_idx) & (ki_idx - qi_idx <= 2 * SW_WIN)
    ones_cols = jnp.ones((3 * blk, LANES), BF16)
    for i in range(tq // blk):
        kpos = base + (i - 1) * blk + ki_idx
        valid = band & (kpos >= 0) & (kpos < seq)
        qi = qs_ref[i * blk:(i + 1) * blk, :]
        outs = []
        for j in range(SW_KV):
            kj = kd_ref[j, i * blk:(i + 3) * blk, :]
            vj = vd_ref[j, i * blk:(i + 3) * blk, :]
            parts = []
            for gi in range(g):
                h = g * j + gi
                qp = qi[:, (h // 2) * LANES:(h // 2 + 1) * LANES]
                keep = lane_lo if h % 2 == 0 else jnp.logical_not(lane_lo)
                parts.append(jnp.where(keep, qp, jnp.zeros_like(qp)))
            s_all = lax.dot_general(jnp.concatenate(parts, axis=0), kj, (((1,), (1,)), ((), ())),
                                    preferred_element_type=F32)
            ps, sink_p = [], []
            for gi in range(g):
                sink = sink_ref[g * j + gi]
                s = jnp.where(valid, s_all[gi * blk:(gi + 1) * blk], NEG_INF)
                m = jnp.maximum(jnp.max(s, axis=-1, keepdims=True), sink)
                ps.append(jnp.exp(s - m).astype(BF16))
                sink_p.append(jnp.exp(sink - m))
            o = jnp.dot(jnp.concatenate(ps, axis=0), jnp.concatenate([vj, ones_cols], axis=1),
                        preferred_element_type=F32)
            outs.append(o[:, :LANES] / (o[:, LANES:] + jnp.concatenate(sink_p, axis=0)))
        for pr in range(SW_HEADS // 2):
            j, g0 = (2 * pr) // g, (2 * pr) % g
            pair = jnp.where(lane_lo, outs[j][g0 * blk:(g0 + 1) * blk], outs[j][(g0 + 1) * blk:(g0 + 2) * blk])
            o_ref[i * blk:(i + 1) * blk, pr * LANES:(pr + 1) * LANES] = pair.astype(o_ref.dtype)


def _rope_tables(seq):
    half = HEAD_DIM // 2
    inv = jnp.power(ROPE_THETA, -jnp.arange(half, dtype=F32) / half)
    ang = jnp.arange(seq, dtype=F32)[:, None] * inv[None, :]
    reps = LANES // half
    return jnp.tile(jnp.cos(ang), (1, reps)), jnp.tile(jnp.sin(ang), (1, reps))


def _swa(z, sink, *, batch, seq, tq):
    n = z.shape[0]
    blk = SW_WIN
    nt = seq // tq
    r = tq // blk
    cos_t, sin_t = _rope_tables(seq)
    kw = SW_KV * HEAD_DIM
    kcol, vcol = Z_SW_K // kw, Z_SW_V // kw
    nblk = n // blk

    def main(col):
        return lambda b, t: (b * nt + t, col)

    def prev(col):
        return lambda b, t: (jnp.maximum((b * nt + t) * r - 1, 0), col)

    def nxt(col):
        return lambda b, t: (jnp.minimum((b * nt + t + 1) * r, nblk - 1), col)

    const = lambda b, t: (0, 0)
    return pl.pallas_call(
        functools.partial(_swa_kernel, tq=tq, seq=seq),
        grid=(batch, nt),
        in_specs=[
            pl.BlockSpec(memory_space=pltpu.SMEM),
            pl.BlockSpec((tq, BR_W), main(Z_SW_Q // BR_W)),
            pl.BlockSpec((blk, kw), prev(kcol)),
            pl.BlockSpec((tq, kw), main(kcol)),
            pl.BlockSpec((blk, kw), nxt(kcol)),
            pl.BlockSpec((blk, kw), prev(vcol)),
            pl.BlockSpec((tq, kw), main(vcol)),
            pl.BlockSpec((blk, kw), nxt(vcol)),
            pl.BlockSpec((seq, LANES), const),
            pl.BlockSpec((seq, LANES), const),
        ],
        out_specs=pl.BlockSpec((tq, BR_W), lambda b, t: (b * nt + t, 0)),
        out_shape=jax.ShapeDtypeStruct((n, BR_W), BF16),
        scratch_shapes=[
            pltpu.VMEM((tq, BR_W), BF16),
            pltpu.VMEM((SW_KV, tq + 2 * blk, LANES), BF16),
            pltpu.VMEM((SW_KV, tq + 2 * blk, LANES), BF16),
        ],
        compiler_params=_params("parallel", "parallel"),
        name="swa",
    )(sink, z, z, z, z, z, z, z, cos_t, sin_t)


def _convmod_kernel(a0_ref, a1_ref, g0_ref, g1_ref, ap0_ref, ap1_ref, gp0_ref, gp1_ref, an0_ref, an1_ref,
                    gn0_ref, gn1_ref, dw_ref, dwb_ref, lng_ref, lnb_ref, o_ref, xs_ref, *, tm, nt, rc):
    t = pl.program_id(1)
    hal = CONV_HALO
    rows = tm + 2 * hal

    def glu(a0, a1, g0, g1):
        a = jnp.concatenate([a0[...], a1[...]], axis=1).astype(F32)
        return a * jax.nn.sigmoid(jnp.concatenate([g0[...], g1[...]], axis=1).astype(F32))

    xs_ref[0, 0:hal, :] = glu(ap0_ref, ap1_ref, gp0_ref, gp1_ref) * (t > 0).astype(F32)
    xs_ref[0, hal:hal + tm, :] = glu(a0_ref, a1_ref, g0_ref, g1_ref)
    xs_ref[0, hal + tm:rows, :] = glu(an0_ref, an1_ref, gn0_ref, gn1_ref) * (t < nt - 1).astype(F32)
    for r0 in range(0, rows - SUBLANES, rc):
        nr = min(rc, rows - SUBLANES - r0)
        x = xs_ref[0, r0:r0 + nr + SUBLANES, :]
        for s in range(1, SUBLANES):
            xs_ref[s, r0:r0 + nr, :] = pltpu.roll(x, nr + SUBLANES - s, 0)[0:nr]
    pad = CONV_W // 2
    for c in range(tm // rc):
        r0 = c * rc
        acc = jnp.zeros((rc, BR_W), F32)
        for k in range(CONV_W):
            off = hal - pad + k
            al = r0 + (off // SUBLANES) * SUBLANES
            acc = acc + dw_ref[k:k + 1, :] * xs_ref[off % SUBLANES, al:al + rc, :]
        y = _layernorm_rows(acc + dwb_ref[...], lng_ref[...], lnb_ref[...])
        o_ref[r0:r0 + rc, :] = (y * jax.nn.sigmoid(y)).astype(o_ref.dtype)


def _halo_specs(width, col, *, nt, tm, halo, nrows):
    r = tm // halo
    last = nrows // halo - 1
    return (
        pl.BlockSpec((tm, width), lambda b, t: (b * nt + t, col)),
        pl.BlockSpec((halo, width), lambda b, t: (jnp.maximum((b * nt + t) * r - 1, 0), col)),
        pl.BlockSpec((halo, width), lambda b, t: (jnp.minimum((b * nt + t + 1) * r, last), col)),
    )


def _convmod(z, dw, dwb, ln_g, ln_b, *, batch, seq, tm):
    n = z.shape[0]
    nt = seq // tm
    groups = [_halo_specs(HALF_W, off // HALF_W + i, nt=nt, tm=tm, halo=CONV_HALO, nrows=n)
              for off in (Z_CV_A, Z_CV_G) for i in range(2)]
    specs = [grp[pos] for pos in range(3) for grp in groups]
    const = lambda b, t: (0, 0)
    row = lambda v: v.reshape(1, BR_W)
    return pl.pallas_call(
        functools.partial(_convmod_kernel, tm=tm, nt=nt, rc=64),
        grid=(batch, nt),
        in_specs=specs + [
            pl.BlockSpec((CONV_W, BR_W), const),
            pl.BlockSpec((1, BR_W), const),
            pl.BlockSpec((1, BR_W), const),
            pl.BlockSpec((1, BR_W), const)],
        out_specs=pl.BlockSpec((tm, BR_W), lambda b, t: (b * nt + t, 0)),
        out_shape=jax.ShapeDtypeStruct((n, BR_W), BF16),
        scratch_shapes=[pltpu.VMEM((SUBLANES, tm + 2 * CONV_HALO, BR_W), F32)],
        compiler_params=_params("parallel", "parallel"),
        name="convmod",
    )(*([z] * len(specs)), dw, row(dwb), row(ln_g), row(ln_b))


def _na_kernel(q0_ref, q1_ref, k0_ref, k1_ref, v0_ref, v1_ref, bias_ref, o_ref, *, rb, rows):
    lane = lax.broadcasted_iota(jnp.int32, (1, LANES), 1)
    lane_lo = lane < HEAD_DIM
    row0 = pl.program_id(1) * rb
    npair = NA_KROWS // 2
    ones_cols = jnp.ones((NA_KROWS * GRID_W, LANES), BF16)

    def one_group(gi, carry):
        r0 = row0 + gi * NA_QROWS
        ks = jnp.clip(r0 - NA_ROWS // 2, 0, rows - NA_KROWS)
        q_rows = pl.ds(pl.multiple_of(gi * (NA_QROWS * GRID_W), NA_QROWS * GRID_W), NA_QROWS * GRID_W)
        k_rows = pl.ds(pl.multiple_of(ks * GRID_W, GRID_W), NA_KROWS * GRID_W)
        entry = []
        for qi in range(NA_QROWS):
            r = r0 + qi
            rs = jnp.clip(r - NA_ROWS // 2, 0, rows - NA_ROWS)
            for w in range(npair):
                kr = ks + 2 * w
                d = kr - r + NA_ROWS - 1
                in0 = (kr >= rs) & (kr < rs + NA_ROWS)
                in1 = (kr + 1 >= rs) & (kr + 1 < rs + NA_ROWS)
                entry.append(jnp.where(in0 & in1, d,
                                       jnp.where(in1, NA_T_LOW + d + 1,
                                                 jnp.where(in0, NA_T_HIGH + d - (NA_ROWS - 1), NA_T_NONE))))
        for pr in range(NA_HEADS // 2):
            cols = slice(pr * LANES, (pr + 1) * LANES)
            hcols = slice((pr % 2) * LANES, (pr % 2 + 1) * LANES)
            qp = (q0_ref, q1_ref)[pr // 2][q_rows, hcols]
            kp = (k0_ref, k1_ref)[pr // 2][k_rows, hcols]
            vp = (v0_ref, v1_ref)[pr // 2][k_rows, hcols]
            halves = []
            for hh in range(2):
                h = 2 * pr + hh
                keep = lane_lo if hh == 0 else jnp.logical_not(lane_lo)
                qm = jnp.where(keep, qp, jnp.zeros_like(qp))
                s = lax.dot_general(qm, kp, (((1,), (1,)), ((), ())), preferred_element_type=F32)
                bias = jnp.concatenate([
                    jnp.concatenate([bias_ref[h, entry[qi * npair + w]] for w in range(npair)], axis=1)
                    for qi in range(NA_QROWS)], axis=0)
                s = s + bias
                m = jnp.max(s, axis=-1, keepdims=True)
                p = jnp.exp(s - m).astype(BF16)
                pv = jnp.dot(p, jnp.concatenate([vp, ones_cols], axis=1), preferred_element_type=F32)
                halves.append(pv[:, :LANES] / pv[:, LANES:])
            o_ref[q_rows, cols] = jnp.where(lane_lo, halves[0], halves[1]).astype(o_ref.dtype)
        return carry

    lax.fori_loop(0, rb // NA_QROWS, one_group, 0)


def _na_bias_table(rpb):
    c = np.arange(GRID_W)
    cs = np.clip(c - NA_COLS // 2, 0, GRID_W - NA_COLS)
    kc = np.arange(GRID_W)
    ok = (kc[None, :] >= cs[:, None]) & (kc[None, :] < cs[:, None] + NA_COLS)
    dc = np.clip(kc[None, :] - c[:, None], 1 - NA_COLS, NA_COLS - 1) + NA_COLS - 1
    full = jnp.where(ok[None, None], rpb.astype(F32)[:, :, dc], NEG_INF)
    out = jnp.full_like(full[:, :NA_ROWS], NEG_INF)
    return jnp.concatenate([
        jnp.concatenate([full[:, :-1], full[:, 1:]], axis=-1),
        jnp.concatenate([out, full[:, :NA_ROWS]], axis=-1),
        jnp.concatenate([full[:, NA_ROWS - 1:], out], axis=-1),
        jnp.concatenate([out[:, :1], out[:, :1]], axis=-1)], axis=1)


def _na(z, table, *, batch, seq, rb):
    n = z.shape[0]
    rows = seq // GRID_W
    nrb = rows // rb
    return pl.pallas_call(
        functools.partial(_na_kernel, rb=rb, rows=rows),
        grid=(batch, nrb),
        in_specs=[
            pl.BlockSpec((rb * GRID_W, HALF_W), lambda b, t: (b * nrb + t, Z_NA_Q // HALF_W)),
            pl.BlockSpec((rb * GRID_W, HALF_W), lambda b, t: (b * nrb + t, Z_NA_Q // HALF_W + 1)),
            pl.BlockSpec((seq, HALF_W), lambda b, t: (b, Z_NA_K // HALF_W)),
            pl.BlockSpec((seq, HALF_W), lambda b, t: (b, Z_NA_K // HALF_W + 1)),
            pl.BlockSpec((seq, HALF_W), lambda b, t: (b, Z_NA_V // HALF_W)),
            pl.BlockSpec((seq, HALF_W), lambda b, t: (b, Z_NA_V // HALF_W + 1)),
            pl.BlockSpec(table.shape, lambda b, t: (0, 0, 0, 0), pipeline_mode=pl.Buffered(1)),
        ],
        out_specs=pl.BlockSpec((rb * GRID_W, BR_W), lambda b, t: (b * nrb + t, 0)),
        out_shape=jax.ShapeDtypeStruct((n, BR_W), BF16),
        compiler_params=_params("parallel", "arbitrary"),
        name="nattn",
    )(z, z, z, z, z, z, table)


def _merge_proj_kernel(oa_ref, ob_ref, oc_ref, od_ref, g_ref, bias_ref, wb_ref, wo_ref, x_ref, o_ref):
    s = pl.program_id(1)
    tm, d = o_ref.shape
    branches = (oa_ref, ob_ref, oc_ref, od_ref)
    per_step = N_BRANCH // 2
    for ss in range(2):
        @pl.when(s == ss)
        def _(ss=ss):
            for r0 in range(0, tm, MXU_DIM):
                rows = slice(r0, r0 + MXU_DIM)
                acc = None
                for kk in range(per_step * ss, per_step * (ss + 1)):
                    gcols = slice((kk % per_step) * d, (kk % per_step + 1) * d)
                    y = jnp.dot(branches[kk][rows, :], wb_ref[kk], preferred_element_type=F32)
                    th = jnp.tanh(0.5 * (g_ref[rows, gcols].astype(F32) + bias_ref[kk:kk + 1, :]))
                    term = (0.5 * y) * (th + 1.0)
                    acc = term if acc is None else acc + term
                if ss == 0:
                    o_ref[rows, :] = acc
                else:
                    merged = (o_ref[rows, :] + acc).astype(BF16)
                    o_ref[rows, :] = x_ref[rows, :] + jnp.dot(merged, wo_ref[...], preferred_element_type=F32)


def _merge_proj(z, branches, gate_b, w_branch, w_out, x, *, tm):
    n, d = x.shape
    per_step = N_BRANCH // 2
    br_spec = pl.BlockSpec((tm, BR_W), lambda i, s: (i, 0))
    return pl.pallas_call(
        _merge_proj_kernel,
        grid=(n // tm, 2),
        in_specs=[br_spec] * N_BRANCH + [
            pl.BlockSpec((tm, per_step * d), lambda i, s: (i, Z_GATES // (per_step * d) + s)),
            pl.BlockSpec((N_BRANCH, d), lambda i, s: (0, 0)),
            pl.BlockSpec((N_BRANCH, BR_W, d), lambda i, s: (0, 0, 0), pipeline_mode=pl.Buffered(1)),
            pl.BlockSpec((d, d), lambda i, s: (0, 0), pipeline_mode=pl.Buffered(1)),
            pl.BlockSpec((tm, d), lambda i, s: (i, 0)),
        ],
        out_specs=pl.BlockSpec((tm, d), lambda i, s: (i, 0)),
        out_shape=jax.ShapeDtypeStruct((n, d), F32),
        compiler_params=_params("parallel", "arbitrary"),
        name="merge_proj",
    )(*branches, z, gate_b, w_branch, w_out, x)


def _ffn_kernel(x_ref, xp_ref, xn_ref, g_ref, wa_ref, wb_ref, dwa_ref, dwb_ref, ba_ref, bb_ref, wd_ref, fg_ref,
                o_ref, h_ref, u0_ref, u1_ref, *, tm, nt, nh, hb, final_norm):
    t = pl.program_id(1)
    j = pl.program_id(2)
    rows = tm + 2 * FFN_HALO
    u_stage = (u0_ref, u1_ref)

    def norm(x):
        ms = jnp.mean(x * x, axis=-1, keepdims=True)
        return x * lax.rsqrt(ms + EPS) * g_ref[...]

    def up(dst_ref):
        h = h_ref[...]
        dst_ref[:, 0:hb] = jnp.dot(h, wa_ref[...], preferred_element_type=F32)
        dst_ref[:, hb:2 * hb] = jnp.dot(h, wb_ref[...], preferred_element_type=F32)

    def down(src_ref):
        part = None
        rc = FFN_ROW_CHUNK
        for c in range(hb // MXU_DIM):
            ca = slice(c * MXU_DIM, (c + 1) * MXU_DIM)
            cb = slice(hb + c * MXU_DIM, hb + (c + 1) * MXU_DIM)
            pair = lambda ra, rb, rows_: jnp.concatenate([ra[rows_, ca], rb[rows_, ca]], axis=1)
            taps = [pair(dwa_ref, dwb_ref, slice(k, k + 1)) for k in range(3)]
            bias = pair(ba_ref, bb_ref, slice(0, 1))
            gated = []
            for r0 in range(0, tm, rc):
                lo = slice(rows - FFN_HALO, rows) if r0 == 0 else slice(r0 - FFN_HALO, r0)
                body = slice(r0, r0 + rc + FFN_HALO)
                u = jnp.concatenate([
                    jnp.concatenate([src_ref[lo, ca], src_ref[body, ca]], axis=0),
                    jnp.concatenate([src_ref[lo, cb], src_ref[body, cb]], axis=0)], axis=1)
                n_u = rc + 2 * FFN_HALO
                keep = slice(FFN_HALO, FFN_HALO + rc)
                ab = (taps[0] * pltpu.roll(u, 1, 0)[keep] + taps[1] * u[keep]
                      + taps[2] * pltpu.roll(u, n_u - 1, 0)[keep] + bias)
                a, b = ab[:, :MXU_DIM], ab[:, MXU_DIM:]
                gated.append(((a * b) * (0.5 * jnp.tanh(0.5 * a) + 0.5)).astype(BF16))
            pc = jnp.dot(jnp.concatenate(gated, axis=0), wd_ref[c * MXU_DIM:(c + 1) * MXU_DIM, :],
                         preferred_element_type=F32)
            part = pc if part is None else part + pc
        o_ref[...] += part

    @pl.when(j == 0)
    def _():
        x = x_ref[...]
        h_ref[0:tm, :] = norm(x).astype(BF16)
        h_ref[tm:rows, :] = jnp.concatenate([
            norm(xn_ref[...]) * (t < nt - 1).astype(F32),
            norm(xp_ref[...]) * (t > 0).astype(F32)], axis=0).astype(BF16)
        o_ref[...] = x
        up(u_stage[0])

    for p in (0, 1):
        @pl.when((j > 0) & (j < nh) & (j % 2 == p))
        def _(p=p):
            up(u_stage[p])
            down(u_stage[1 - p])

    @pl.when(j == nh)
    def _():
        down(u_stage[(nh - 1) % 2])
        if final_norm:
            y = o_ref[...]
            ms = jnp.mean(y * y, axis=-1, keepdims=True)
            o_ref[...] = y * lax.rsqrt(ms + EPS) * fg_ref[...]


def _ffn(x, norm_g, w_up, dw, dwb, w_down, final_g, *, batch, seq, tm, hb, final_norm, single_buffer_x):
    n, d = x.shape
    nt = seq // tm
    nh = FFN_HIDDEN // hb
    r = tm // FFN_HALO
    last = n // FFN_HALO - 1
    const = lambda b, t, j: (0, 0)
    up_a = lambda b, t, j: (0, jnp.minimum(j, nh - 1))
    up_b = lambda b, t, j: (0, nh + jnp.minimum(j, nh - 1))
    gate_a = lambda b, t, j: (0, jnp.maximum(j - 1, 0))
    gate_b = lambda b, t, j: (0, nh + jnp.maximum(j - 1, 0))
    u_stage = pltpu.VMEM((tm + 2 * FFN_HALO, 2 * hb), F32)
    dwb = dwb.reshape(1, -1)
    x_mode = dict(pipeline_mode=pl.Buffered(1)) if single_buffer_x else {}
    return pl.pallas_call(
        functools.partial(_ffn_kernel, tm=tm, nt=nt, nh=nh, hb=hb, final_norm=final_norm),
        grid=(batch, nt, nh + 1),
        in_specs=[
            pl.BlockSpec((tm, d), lambda b, t, j: (b * nt + t, 0), **x_mode),
            pl.BlockSpec((FFN_HALO, d), lambda b, t, j: (jnp.maximum((b * nt + t) * r - 1, 0), 0)),
            pl.BlockSpec((FFN_HALO, d), lambda b, t, j: (jnp.minimum((b * nt + t + 1) * r, last), 0)),
            pl.BlockSpec((1, d), const),
            pl.BlockSpec((d, hb), up_a),
            pl.BlockSpec((d, hb), up_b),
            pl.BlockSpec((3, hb), gate_a),
            pl.BlockSpec((3, hb), gate_b),
            pl.BlockSpec((1, hb), gate_a),
            pl.BlockSpec((1, hb), gate_b),
            pl.BlockSpec((hb, d), lambda b, t, j: (jnp.maximum(j - 1, 0), 0)),
            pl.BlockSpec((1, d), const),
        ],
        out_specs=pl.BlockSpec((tm, d), lambda b, t, j: (b * nt + t, 0)),
        out_shape=jax.ShapeDtypeStruct((n, d), F32),
        scratch_shapes=[pltpu.VMEM((tm + 2 * FFN_HALO, d), BF16), u_stage, u_stage],
        compiler_params=_params("parallel", "parallel", "arbitrary"),
        name="conv_ffn",
    )(x, x, x, norm_g.reshape(1, d), w_up, w_up, dw, dw, dwb, dwb, w_down, final_g.reshape(1, d))


def _permute_w_in(w_in):
    n_rest = Z_COLS - N_BRANCH * D_MODEL
    scale = np.ones((n_rest,), np.float32)
    for off in (Z_SW_Q, Z_NA_Q):
        scale[off - Z_GM_U:off - Z_GM_U + BR_W] = HEAD_DIM ** -0.5
    return jnp.concatenate([w_in[:, :, n_rest:], w_in[:, :, :n_rest] * scale], axis=-1).astype(BF16)


def _trunk(x, p, *, tm=512):
    batch, seq, d = x.shape
    n = batch * seq
    x2 = x.reshape(n, d)
    for l in range(DEPTH):
        z = _norm_proj(x2, p["norm1_g"][l], p["w_in"][l], tm=2 * tm, tn=1792)
        o_a = _gmlp(z, p["gm_ln_g"][l], p["gm_ln_b"][l], p["gm_ws"][l], p["gm_bs"][l], tm=tm)
        o_b = _swa(z, p["sw_sink"][l], batch=batch, seq=seq, tq=tm)
        o_c = _convmod(z, p["cv_dw"][l], p["cv_dwb"][l], p["cv_ln_g"][l], p["cv_ln_b"][l],
                       batch=batch, seq=seq, tm=tm)
        o_d = _na(z, p["na_table"][l], batch=batch, seq=seq, rb=16)
        x2 = _merge_proj(z, (o_a, o_b, o_c, o_d), p["gate_b"][l], p["w_branch"][l], p["w_out"][l], x2, tm=tm)
        x2 = _ffn(x2, p["norm2_g"][l], p["w_up"][l], p["ffn_dw"][l], p["ffn_dwb"][l], p["w_down"][l],
                  p["final_g"], batch=batch, seq=seq, tm=2 * tm, hb=512, final_norm=(l == DEPTH - 1),
                  single_buffer_x=True)
    return x2.reshape(batch, seq, d)


def kernel(x_prompt, x_sample, norm1_g, w_in, gate_b, gm_ln_g, gm_ln_b, gm_ws, gm_bs, sw_sink, cv_dw, cv_dwb,
           cv_ln_g, cv_ln_b, na_rpb, w_branch, w_out, norm2_g, w_up, ffn_dw, ffn_dwb, w_down, final_g):
    p = dict(norm1_g=norm1_g, w_in=_permute_w_in(w_in), gate_b=gate_b, gm_ln_g=gm_ln_g, gm_ln_b=gm_ln_b,
             gm_ws=gm_ws, gm_bs=gm_bs, sw_sink=sw_sink, cv_dw=cv_dw, cv_dwb=cv_dwb, cv_ln_g=cv_ln_g,
             cv_ln_b=cv_ln_b, na_table=[_na_bias_table(na_rpb[l]) for l in range(DEPTH)],
             w_branch=w_branch.astype(BF16), w_out=w_out.astype(BF16),
             norm2_g=norm2_g, w_up=w_up.astype(BF16), ffn_dw=ffn_dw, ffn_dwb=ffn_dwb,
             w_down=w_down.astype(BF16), final_g=final_g)
    return (_trunk(x_prompt, p), _trunk(x_sample, p))
```

```python
import functools

import numpy as np
import jax
import jax.numpy as jnp
from jax import lax
from jax.experimental import pallas as pl
from jax.experimental.pallas import tpu as pltpu

F32 = jnp.float32
BF16 = jnp.bfloat16

D_MODEL = 2048
DEPTH = 2
GRID_W = 64
N_BRANCH = 4
BR_W = 512
HEAD_DIM = 64
CHUNK = 128
GM_GROUPS = 4
SW_HEADS = 8
SW_KV = 2
SW_WIN = 128
ROPE_THETA = 10000.0
CONV_W = 31
NA_HEADS = 8
NA_ROWS = 8
NA_COLS = 16
NA_QROWS = 4
NA_KROWS = 12
NA_T_LOW = 2 * NA_ROWS - 2
NA_T_HIGH = NA_T_LOW + NA_ROWS
NA_T_NONE = NA_T_HIGH + NA_ROWS
FFN_HIDDEN = 5632
EPS = 1e-6
NEG_INF = -1e30

LANES = 128
SUBLANES = 8
MXU_DIM = 256
CONV_HALO = 16
FFN_HALO = SUBLANES
FFN_ROW_CHUNK = 256

Z_GATES = 0
Z_GM_U = 8192
Z_GM_V = 8704
Z_SW_Q = 9216
Z_SW_K = 9728
Z_SW_V = 9856
Z_CV_A = 9984
Z_CV_G = 10496
Z_NA_Q = 11008
Z_NA_K = 11520
Z_NA_V = 12032
Z_COLS = 12544
HALF_W = BR_W // 2

VMEM_LIMIT = 56 * 1024 * 1024


def _params(*sem):
    return pltpu.CompilerParams(dimension_semantics=sem, vmem_limit_bytes=VMEM_LIMIT)


def _norm_proj_kernel(x_ref, g_ref, w_ref, o_ref, h_ref):
    @pl.when(pl.program_id(1) == 0)
    def _():
        x = x_ref[...]
        ms = jnp.mean(x * x, axis=-1, keepdims=True)
        h_ref[...] = (x * lax.rsqrt(ms + EPS) * g_ref[...]).astype(BF16)

    o_ref[...] = jnp.dot(h_ref[...], w_ref[...], preferred_element_type=F32).astype(o_ref.dtype)


def _norm_proj(x, g, w, *, tm, tn):
    n, d = x.shape
    nout = w.shape[1]
    return pl.pallas_call(
        _norm_proj_kernel,
        grid=(n // tm, nout // tn),
        in_specs=[
            pl.BlockSpec((tm, d), lambda i, j: (i, 0)),
            pl.BlockSpec((1, d), lambda i, j: (0, 0)),
            pl.BlockSpec((d, tn), lambda i, j: (0, j)),
        ],
        out_specs=pl.BlockSpec((tm, tn), lambda i, j: (i, j)),
        out_shape=jax.ShapeDtypeStruct((n, nout), BF16),
        scratch_shapes=[pltpu.VMEM((tm, d), BF16)],
        compiler_params=_params("parallel", "arbitrary"),
        name="norm_proj",
    )(x, g.reshape(1, d), w)


def _layernorm_rows(x, g, b):
    mu = jnp.mean(x, axis=-1, keepdims=True)
    xc = x - mu
    var = jnp.mean(xc * xc, axis=-1, keepdims=True)
    return xc * lax.rsqrt(var + EPS) * g + b


def _gmlp_kernel(u_ref, v_ref, lng_ref, lnb_ref, ws_ref, bs_ref, o_ref, *, tm):
    vn = _layernorm_rows(v_ref[...].astype(F32), lng_ref[...], lnb_ref[...]).astype(BF16)
    gw = BR_W // GM_GROUPS
    for c in range(tm // CHUNK):
        rows = slice(c * CHUNK, (c + 1) * CHUNK)
        for g in range(GM_GROUPS):
            cols = slice(g * gw, (g + 1) * gw)
            mixed = jnp.dot(ws_ref[g], vn[rows, cols], preferred_element_type=F32) + bs_ref[:, cols]
            o_ref[rows, cols] = (u_ref[rows, cols].astype(F32) * mixed).astype(o_ref.dtype)


def _gmlp(z, ln_g, ln_b, ws, bs, *, tm):
    n = z.shape[0]
    gw = BR_W // GM_GROUPS
    bs_full = jnp.repeat(bs.T, gw, axis=1)
    const = lambda i: (0, 0)
    return pl.pallas_call(
        functools.partial(_gmlp_kernel, tm=tm),
        grid=(n // tm,),
        in_specs=[
            pl.BlockSpec((tm, BR_W), lambda i: (i, Z_GM_U // BR_W)),
            pl.BlockSpec((tm, BR_W), lambda i: (i, Z_GM_V // BR_W)),
            pl.BlockSpec((1, BR_W), const),
            pl.BlockSpec((1, BR_W), const),
            pl.BlockSpec((GM_GROUPS, CHUNK, CHUNK), lambda i: (0, 0, 0)),
            pl.BlockSpec((CHUNK, BR_W), const),
        ],
        out_specs=pl.BlockSpec((tm, BR_W), lambda i: (i, 0)),
        out_shape=jax.ShapeDtypeStruct((n, BR_W), BF16),
        compiler_params=_params("parallel"),
        name="gmlp",
    )(z, z, ln_g.reshape(1, BR_W), ln_b.reshape(1, BR_W), ws.astype(BF16), bs_full)


def _rope(x, cos, sin, first_half):
    w = x.shape[1]
    reps = w // LANES
    if reps > 1:
        cos = jnp.concatenate([cos] * reps, axis=1)
        sin = jnp.concatenate([sin] * reps, axis=1)
        first_half = jnp.concatenate([first_half] * reps, axis=1)
    half = HEAD_DIM // 2
    lower = pltpu.roll(x, half, 1)
    upper = pltpu.roll(x, w - half, 1)
    return x * cos + jnp.where(first_half, -upper, lower) * sin


def _swa_kernel(sink_ref, q_ref, kp_ref, k_ref, kn_ref, vp_ref, v_ref, vn_ref, cos_ref, sin_ref,
                o_ref, qs_ref, kd_ref, vd_ref, *, tq, seq):
    t = pl.program_id(1)
    base = t * tq
    blk = SW_WIN
    lane = lax.broadcasted_iota(jnp.int32, (1, LANES), 1)
    first_half = (lane % HEAD_DIM) < (HEAD_DIM // 2)
    lane_lo = lane < HEAD_DIM

    def table(ref, start, size):
        return ref[pl.ds(pl.multiple_of(start, blk), size), :]

    p_start = jnp.maximum(base - blk, 0)
    n_start = jnp.minimum(base + tq, seq - blk)
    cq, sq = table(cos_ref, base, tq), table(sin_ref, base, tq)
    qs_ref[...] = _rope(q_ref[...].astype(F32), cq, sq, first_half).astype(BF16)
    k_ext = jnp.concatenate([
        _rope(kp_ref[...].astype(F32), table(cos_ref, p_start, blk), table(sin_ref, p_start, blk), first_half),
        _rope(k_ref[...].astype(F32), cq, sq, first_half),
        _rope(kn_ref[...].astype(F32), table(cos_ref, n_start, blk), table(sin_ref, n_start, blk), first_half),
    ], axis=0)
    v_ext = jnp.concatenate([vp_ref[...], v_ref[...], vn_ref[...]], axis=0).astype(F32)
    for src, dst in ((k_ext, kd_ref), (v_ext, vd_ref)):
        swapped = pltpu.roll(src, HEAD_DIM, 1)
        dst[0] = jnp.where(lane_lo, src, swapped).astype(BF16)
        dst[1] = jnp.where(lane_lo, swapped, src).astype(BF16)

    g = SW_HEADS // SW_KV
    qi_idx = lax.broadcasted_iota(jnp.int32, (blk, 3 * blk), 0)
    ki_idx = lax.broadcasted_iota(jnp.int32, (blk, 3 * blk), 1)
    band = (ki_idx >= qi_idx) & (ki_idx - qi_idx <= 2 * SW_WIN)
    ones_cols = jnp.ones((3 * blk, LANES), BF16)
    for i in range(tq // blk):
        kpos = base + (i - 1) * blk + ki_idx
        valid = band & (kpos >= 0) & (kpos < seq)
        qi = qs_ref[i * blk:(i + 1) * blk, :]
        outs = []
        for j in range(SW_KV):
            kj = kd_ref[j, i * blk:(i + 3) * blk, :]
            vj = vd_ref[j, i * blk:(i + 3) * blk, :]
            parts = []
            for gi in range(g):
                h = g * j + gi
                qp = qi[:, (h // 2) * LANES:(h // 2 + 1) * LANES]
                keep = lane_lo if h % 2 == 0 else jnp.logical_not(lane_lo)
                parts.append(jnp.where(keep, qp, jnp.zeros_like(qp)))
            s_all = lax.dot_general(jnp.concatenate(parts, axis=0), kj, (((1,), (1,)), ((), ())),
                                    preferred_element_type=F32)
            ps, sink_p = [], []
            for gi in range(g):
                sink = sink_ref[g * j + gi]
                s = jnp.where(valid, s_all[gi * blk:(gi + 1) * blk], NEG_INF)
                m = jnp.maximum(jnp.max(s, axis=-1, keepdims=True), sink)
                ps.append(jnp.exp(s - m).astype(BF16))
                sink_p.append(jnp.exp(sink - m))
            o = jnp.dot(jnp.concatenate(ps, axis=0), jnp.concatenate([vj, ones_cols], axis=1),
                        preferred_element_type=F32)
            outs.append(o[:, :LANES] / (o[:, LANES:] + jnp.concatenate(sink_p, axis=0)))
        for pr in range(SW_HEADS // 2):
            j, g0 = (2 * pr) // g, (2 * pr) % g
            pair = jnp.where(lane_lo, outs[j][g0 * blk:(g0 + 1) * blk], outs[j][(g0 + 1) * blk:(g0 + 2) * blk])
            o_ref[i * blk:(i + 1) * blk, pr * LANES:(pr + 1) * LANES] = pair.astype(o_ref.dtype)


def _rope_tables(seq):
    half = HEAD_DIM // 2
    inv = jnp.power(ROPE_THETA, -jnp.arange(half, dtype=F32) / half)
    ang = jnp.arange(seq, dtype=F32)[:, None] * inv[None, :]
    reps = LANES // half
    return jnp.tile(jnp.cos(ang), (1, reps)), jnp.tile(jnp.sin(ang), (1, reps))


def _swa(z, sink, *, batch, seq, tq):
    n = z.shape[0]
    blk = SW_WIN
    nt = seq // tq
    r = tq // blk
    cos_t, sin_t = _rope_tables(seq)
    kw = SW_KV * HEAD_DIM
    kcol, vcol = Z_SW_K // kw, Z_SW_V // kw
    nblk = n // blk

    def main(col):
        return lambda b, t: (b * nt + t, col)

    def prev(col):
        return lambda b, t: (jnp.maximum((b * nt + t) * r - 1, 0), col)

    def nxt(col):
        return lambda b, t: (jnp.minimum((b * nt + t + 1) * r, nblk - 1), col)

    const = lambda b, t: (0, 0)
    return pl.pallas_call(
        functools.partial(_swa_kernel, tq=tq, seq=seq),
        grid=(batch, nt),
        in_specs=[
            pl.BlockSpec(memory_space=pltpu.SMEM),
            pl.BlockSpec((tq, BR_W), main(Z_SW_Q // BR_W)),
            pl.BlockSpec((blk, kw), prev(kcol)),
            pl.BlockSpec((tq, kw), main(kcol)),
            pl.BlockSpec((blk, kw), nxt(kcol)),
            pl.BlockSpec((blk, kw), prev(vcol)),
            pl.BlockSpec((tq, kw), main(vcol)),
            pl.BlockSpec((blk, kw), nxt(vcol)),
            pl.BlockSpec((seq, LANES), const),
            pl.BlockSpec((seq, LANES), const),
        ],
        out_specs=pl.BlockSpec((tq, BR_W), lambda b, t: (b * nt + t, 0)),
        out_shape=jax.ShapeDtypeStruct((n, BR_W), BF16),
        scratch_shapes=[
            pltpu.VMEM((tq, BR_W), BF16),
            pltpu.VMEM((SW_KV, tq + 2 * blk, LANES), BF16),
            pltpu.VMEM((SW_KV, tq + 2 * blk, LANES), BF16),
        ],
        compiler_params=_params("parallel", "parallel"),
        name="swa",
    )(sink, z, z, z, z, z, z, z, cos_t, sin_t)


def _convmod_kernel(a0_ref, a1_ref, g0_ref, g1_ref, ap0_ref, ap1_ref, gp0_ref, gp1_ref, an0_ref, an1_ref,
                    gn0_ref, gn1_ref, dw_ref, dwb_ref, lng_ref, lnb_ref, o_ref, xs_ref, *, tm, nt, rc):
    t = pl.program_id(1)
    hal = CONV_HALO
    rows = tm + 2 * hal

    def glu(a0, a1, g0, g1):
        a = jnp.concatenate([a0[...], a1[...]], axis=1).astype(F32)
        return a * jax.nn.sigmoid(jnp.concatenate([g0[...], g1[...]], axis=1).astype(F32))

    xs_ref[0, 0:hal, :] = glu(ap0_ref, ap1_ref, gp0_ref, gp1_ref) * (t > 0).astype(F32)
    xs_ref[0, hal:hal + tm, :] = glu(a0_ref, a1_ref, g0_ref, g1_ref)
    xs_ref[0, hal + tm:rows, :] = glu(an0_ref, an1_ref, gn0_ref, gn1_ref) * (t < nt - 1).astype(F32)
    for r0 in range(0, rows - SUBLANES, rc):
        nr = min(rc, rows - SUBLANES - r0)
        x = xs_ref[0, r0:r0 + nr + SUBLANES, :]
        for s in range(1, SUBLANES):
            xs_ref[s, r0:r0 + nr, :] = pltpu.roll(x, nr + SUBLANES - s, 0)[0:nr]
    pad = CONV_W // 2
    for c in range(tm // rc):
        r0 = c * rc
        acc = jnp.zeros((rc, BR_W), F32)
        for k in range(CONV_W):
            off = hal - pad + k
            al = r0 + (off // SUBLANES) * SUBLANES
            acc = acc + dw_ref[k:k + 1, :] * xs_ref[off % SUBLANES, al:al + rc, :]
        y = _layernorm_rows(acc + dwb_ref[...], lng_ref[...], lnb_ref[...])
        o_ref[r0:r0 + rc, :] = (y * jax.nn.sigmoid(y)).astype(o_ref.dtype)


def _halo_specs(width, col, *, nt, tm, halo, nrows):
    r = tm // halo
    last = nrows // halo - 1
    return (
        pl.BlockSpec((tm, width), lambda b, t: (b * nt + t, col)),
        pl.BlockSpec((halo, width), lambda b, t: (jnp.maximum((b * nt + t) * r - 1, 0), col)),
        pl.BlockSpec((halo, width), lambda b, t: (jnp.minimum((b * nt + t + 1) * r, last), col)),
    )


def _convmod(z, dw, dwb, ln_g, ln_b, *, batch, seq, tm):
    n = z.shape[0]
    nt = seq // tm
    groups = [_halo_specs(HALF_W, off // HALF_W + i, nt=nt, tm=tm, halo=CONV_HALO, nrows=n)
              for off in (Z_CV_A, Z_CV_G) for i in range(2)]
    specs = [grp[pos] for pos in range(3) for grp in groups]
    const = lambda b, t: (0, 0)
    row = lambda v: v.reshape(1, BR_W)
    return pl.pallas_call(
        functools.partial(_convmod_kernel, tm=tm, nt=nt, rc=64),
        grid=(batch, nt),
        in_specs=specs + [
            pl.BlockSpec((CONV_W, BR_W), const),
            pl.BlockSpec((1, BR_W), const),
            pl.BlockSpec((1, BR_W), const),
            pl.BlockSpec((1, BR_W), const)],
        out_specs=pl.BlockSpec((tm, BR_W), lambda b, t: (b * nt + t, 0)),
        out_shape=jax.ShapeDtypeStruct((n, BR_W), BF16),
        scratch_shapes=[pltpu.VMEM((SUBLANES, tm + 2 * CONV_HALO, BR_W), F32)],
        compiler_params=_params("parallel", "parallel"),
        name="convmod",
    )(*([z] * len(specs)), dw, row(dwb), row(ln_g), row(ln_b))


def _na_kernel(q0_ref, q1_ref, k0_ref, k1_ref, v0_ref, v1_ref, bias_ref, o_ref, *, rb, rows):
    lane = lax.broadcasted_iota(jnp.int32, (1, LANES), 1)
    lane_lo = lane < HEAD_DIM
    row0 = pl.program_id(1) * rb
    npair = NA_KROWS // 2
    ones_cols = jnp.ones((NA_KROWS * GRID_W, LANES), BF16)

    def one_group(gi, carry):
        r0 = row0 + gi * NA_QROWS
        ks = jnp.clip(r0 - NA_ROWS // 2, 0, rows - NA_KROWS)
        q_rows = pl.ds(pl.multiple_of(gi * (NA_QROWS * GRID_W), NA_QROWS * GRID_W), NA_QROWS * GRID_W)
        k_rows = pl.ds(pl.multiple_of(ks * GRID_W, GRID_W), NA_KROWS * GRID_W)
        entry = []
        for qi in range(NA_QROWS):
            r = r0 + qi
            rs = jnp.clip(r - NA_ROWS // 2, 0, rows - NA_ROWS)
            for w in range(npair):
                kr = ks + 2 * w
                d = kr - r + NA_ROWS - 1
                in0 = (kr >= rs) & (kr < rs + NA_ROWS)
                in1 = (kr + 1 >= rs) & (kr + 1 < rs + NA_ROWS)
                entry.append(jnp.where(in0 & in1, d,
                                       jnp.where(in1, NA_T_LOW + d + 1,
                                                 jnp.where(in0, NA_T_HIGH + d - (NA_ROWS - 1), NA_T_NONE))))
        for pr in range(NA_HEADS // 2):
            cols = slice(pr * LANES, (pr + 1) * LANES)
            hcols = slice((pr % 2) * LANES, (pr % 2 + 1) * LANES)
            qp = (q0_ref, q1_ref)[pr // 2][q_rows, hcols]
            kp = (k0_ref, k1_ref)[pr // 2][k_rows, hcols]
            vp = (v0_ref, v1_ref)[pr // 2][k_rows, hcols]
            halves = []
            for hh in range(2):
                h = 2 * pr + hh
                keep = lane_lo if hh == 0 else jnp.logical_not(lane_lo)
                qm = jnp.where(keep, qp, jnp.zeros_like(qp))
                s = lax.dot_general(qm, kp, (((1,), (1,)), ((), ())), preferred_element_type=F32)
                bias = jnp.concatenate([
                    jnp.concatenate([bias_ref[h, entry[qi * npair + w]] for w in range(npair)], axis=1)
                    for qi in range(NA_QROWS)], axis=0)
                s = s + bias
                m = jnp.max(s, axis=-1, keepdims=True)
                p = jnp.exp(s - m).astype(BF16)
                pv = jnp.dot(p, jnp.concatenate([vp, ones_cols], axis=1), preferred_element_type=F32)
                halves.append(pv[:, :LANES] / pv[:, LANES:])
            o_ref[q_rows, cols] = jnp.where(lane_lo, halves[0], halves[1]).astype(o_ref.dtype)
        return carry

    lax.fori_loop(0, rb // NA_QROWS, one_group, 0)


def _na_bias_table(rpb):
    c = np.arange(GRID_W)
    cs = np.clip(c - NA_COLS // 2, 0, GRID_W - NA_COLS)
    kc = np.arange(GRID_W)
    ok = (kc[None, :] >= cs[:, None]) & (kc[None, :] < cs[:, None] + NA_COLS)
    dc = np.clip(kc[None, :] - c[:, None], 1 - NA_COLS, NA_COLS - 1) + NA_COLS - 1
    full = jnp.where(ok[None, None], rpb.astype(F32)[:, :, dc], NEG_INF)
    out = jnp.full_like(full[:, :NA_ROWS], NEG_INF)
    return jnp.concatenate([
        jnp.concatenate([full[:, :-1], full[:, 1:]], axis=-1),
        jnp.concatenate([out, full[:, :NA_ROWS]], axis=-1),
        jnp.concatenate([full[:, NA_ROWS - 1:], out], axis=-1),
        jnp.concatenate([out[:, :1], out[:, :1]], axis=-1)], axis=1)


def _na(z, table, *, batch, seq, rb):
    n = z.shape[0]
    rows = seq // GRID_W
    nrb = rows // rb
    return pl.pallas_call(
        functools.partial(_na_kernel, rb=rb, rows=rows),
        grid=(batch, nrb),
        in_specs=[
            pl.BlockSpec((rb * GRID_W, HALF_W), lambda b, t: (b * nrb + t, Z_NA_Q // HALF_W)),
            pl.BlockSpec((rb * GRID_W, HALF_W), lambda b, t: (b * nrb + t, Z_NA_Q // HALF_W + 1)),
            pl.BlockSpec((seq, HALF_W), lambda b, t: (b, Z_NA_K // HALF_W)),
            pl.BlockSpec((seq, HALF_W), lambda b, t: (b, Z_NA_K // HALF_W + 1)),
            pl.BlockSpec((seq, HALF_W), lambda b, t: (b, Z_NA_V // HALF_W)),
            pl.BlockSpec((seq, HALF_W), lambda b, t: (b, Z_NA_V // HALF_W + 1)),
            pl.BlockSpec(table.shape, lambda b, t: (0, 0, 0, 0), pipeline_mode=pl.Buffered(1)),
        ],
        out_specs=pl.BlockSpec((rb * GRID_W, BR_W), lambda b, t: (b * nrb + t, 0)),
        out_shape=jax.ShapeDtypeStruct((n, BR_W), BF16),
        compiler_params=_params("parallel", "arbitrary"),
        name="nattn",
    )(z, z, z, z, z, z, table)


def _merge_proj_kernel(oa_ref, ob_ref, oc_ref, od_ref, g_ref, bias_ref, wb_ref, wo_ref, x_ref, o_ref):
    s = pl.program_id(1)
    tm, d = o_ref.shape
    branches = (oa_ref, ob_ref, oc_ref, od_ref)
    per_step = N_BRANCH // 2
    for ss in range(2):
        @pl.when(s == ss)
        def _(ss=ss):
            for r0 in range(0, tm, MXU_DIM):
                rows = slice(r0, r0 + MXU_DIM)
                acc = None
                for kk in range(per_step * ss, per_step * (ss + 1)):
                    gcols = slice((kk % per_step) * d, (kk % per_step + 1) * d)
                    y = jnp.dot(branches[kk][rows, :], wb_ref[kk], preferred_element_type=F32)
                    th = jnp.tanh(g_ref[rows, gcols].astype(F32) + bias_ref[kk:kk + 1, :])
                    term = y * (th + 1.0)
                    acc = term if acc is None else acc + term
                if ss == 0:
                    o_ref[rows, :] = acc
                else:
                    merged = (o_ref[rows, :] + acc).astype(BF16)
                    o_ref[rows, :] = x_ref[rows, :] + jnp.dot(merged, wo_ref[...], preferred_element_type=F32)


def _merge_proj(z, branches, gate_b, w_branch, w_out, x, *, tm):
    n, d = x.shape
    per_step = N_BRANCH // 2
    br_spec = pl.BlockSpec((tm, BR_W), lambda i, s: (i, 0))
    return pl.pallas_call(
        _merge_proj_kernel,
        grid=(n // tm, 2),
        in_specs=[br_spec] * N_BRANCH + [
            pl.BlockSpec((tm, per_step * d), lambda i, s: (i, Z_GATES // (per_step * d) + s)),
            pl.BlockSpec((N_BRANCH, d), lambda i, s: (0, 0)),
            pl.BlockSpec((N_BRANCH, BR_W, d), lambda i, s: (0, 0, 0), pipeline_mode=pl.Buffered(1)),
            pl.BlockSpec((d, d), lambda i, s: (0, 0), pipeline_mode=pl.Buffered(1)),
            pl.BlockSpec((tm, d), lambda i, s: (i, 0)),
        ],
        out_specs=pl.BlockSpec((tm, d), lambda i, s: (i, 0)),
        out_shape=jax.ShapeDtypeStruct((n, d), F32),
        compiler_params=_params("parallel", "arbitrary"),
        name="merge_proj",
    )(*branches, z, gate_b, w_branch, w_out, x)


def _ffn_kernel(x_ref, xp_ref, xn_ref, g_ref, wa_ref, wb_ref, dwa_ref, dwb_ref, ba_ref, bb_ref, wd_ref, fg_ref,
                o_ref, h_ref, u0_ref, u1_ref, *, tm, nt, nh, hb, final_norm):
    t = pl.program_id(1)
    j = pl.program_id(2)
    rows = tm + 2 * FFN_HALO
    u_stage = (u0_ref, u1_ref)

    def norm(x):
        ms = jnp.mean(x * x, axis=-1, keepdims=True)
        return x * lax.rsqrt(ms + EPS) * g_ref[...]

    def up(dst_ref):
        h = h_ref[...]
        dst_ref[:, 0:hb] = jnp.dot(h, wa_ref[...], preferred_element_type=F32)
        dst_ref[:, hb:2 * hb] = jnp.dot(h, wb_ref[...], preferred_element_type=F32)

    def down(src_ref):
        part = None
        rc = FFN_ROW_CHUNK
        for c in range(hb // MXU_DIM):
            ca = slice(c * MXU_DIM, (c + 1) * MXU_DIM)
            cb = slice(hb + c * MXU_DIM, hb + (c + 1) * MXU_DIM)
            pair = lambda ra, rb, rows_: jnp.concatenate([ra[rows_, ca], rb[rows_, ca]], axis=1)
            taps = [pair(dwa_ref, dwb_ref, slice(k, k + 1)) for k in range(3)]
            bias = pair(ba_ref, bb_ref, slice(0, 1))
            gated = []
            for r0 in range(0, tm, rc):
                lo = slice(rows - FFN_HALO, rows) if r0 == 0 else slice(r0 - FFN_HALO, r0)
                body = slice(r0, r0 + rc + FFN_HALO)
                u = jnp.concatenate([
                    jnp.concatenate([src_ref[lo, ca], src_ref[body, ca]], axis=0),
                    jnp.concatenate([src_ref[lo, cb], src_ref[body, cb]], axis=0)], axis=1)
                n_u = rc + 2 * FFN_HALO
                keep = slice(FFN_HALO, FFN_HALO + rc)
                ab = (taps[0] * pltpu.roll(u, 1, 0)[keep] + taps[1] * u[keep]
                      + taps[2] * pltpu.roll(u, n_u - 1, 0)[keep] + bias)
                a, b = ab[:, :MXU_DIM], ab[:, MXU_DIM:]
                gated.append(((a * b) * (jnp.tanh(a) + 1.0)).astype(BF16))
            pc = jnp.dot(jnp.concatenate(gated, axis=0), wd_ref[c * MXU_DIM:(c + 1) * MXU_DIM, :],
                         preferred_element_type=F32)
            part = pc if part is None else part + pc
        o_ref[...] += part

    @pl.when(j == 0)
    def _():
        x = x_ref[...]
        h_ref[0:tm, :] = norm(x).astype(BF16)
        h_ref[tm:rows, :] = jnp.concatenate([
            norm(xn_ref[...]) * (t < nt - 1).astype(F32),
            norm(xp_ref[...]) * (t > 0).astype(F32)], axis=0).astype(BF16)
        o_ref[...] = x
        up(u_stage[0])

    for p in (0, 1):
        @pl.when((j > 0) & (j < nh) & (j % 2 == p))
        def _(p=p):
            up(u_stage[p])
            down(u_stage[1 - p])

    @pl.when(j == nh)
    def _():
        down(u_stage[(nh - 1) % 2])
        if final_norm:
            y = o_ref[...]
            ms = jnp.mean(y * y, axis=-1, keepdims=True)
            o_ref[...] = y * lax.rsqrt(ms + EPS) * fg_ref[...]


def _ffn(x, norm_g, w_up, dw, dwb, w_down, final_g, *, batch, seq, tm, hb, final_norm, single_buffer_x):
    n, d = x.shape
    nt = seq // tm
    nh = FFN_HIDDEN // hb
    r = tm // FFN_HALO
    last = n // FFN_HALO - 1
    const = lambda b, t, j: (0, 0)
    up_a = lambda b, t, j: (0, jnp.minimum(j, nh - 1))
    up_b = lambda b, t, j: (0, nh + jnp.minimum(j, nh - 1))
    gate_a = lambda b, t, j: (0, jnp.maximum(j - 1, 0))
    gate_b = lambda b, t, j: (0, nh + jnp.maximum(j - 1, 0))
    u_stage = pltpu.VMEM((tm + 2 * FFN_HALO, 2 * hb), F32)
    dwb = dwb.reshape(1, -1)
    x_mode = dict(pipeline_mode=pl.Buffered(1)) if single_buffer_x else {}
    return pl.pallas_call(
        functools.partial(_ffn_kernel, tm=tm, nt=nt, nh=nh, hb=hb, final_norm=final_norm),
        grid=(batch, nt, nh + 1),
        in_specs=[
            pl.BlockSpec((tm, d), lambda b, t, j: (b * nt + t, 0), **x_mode),
            pl.BlockSpec((FFN_HALO, d), lambda b, t, j: (jnp.maximum((b * nt + t) * r - 1, 0), 0)),
            pl.BlockSpec((FFN_HALO, d), lambda b, t, j: (jnp.minimum((b * nt + t + 1) * r, last), 0)),
            pl.BlockSpec((1, d), const),
            pl.BlockSpec((d, hb), up_a),
            pl.BlockSpec((d, hb), up_b),
            pl.BlockSpec((3, hb), gate_a),
            pl.BlockSpec((3, hb), gate_b),
            pl.BlockSpec((1, hb), gate_a),
            pl.BlockSpec((1, hb), gate_b),
            pl.BlockSpec((hb, d), lambda b, t, j: (jnp.maximum(j - 1, 0), 0)),
            pl.BlockSpec((1, d), const),
        ],
        out_specs=pl.BlockSpec((tm, d), lambda b, t, j: (b * nt + t, 0)),
        out_shape=jax.ShapeDtypeStruct((n, d), F32),
        scratch_shapes=[pltpu.VMEM((tm + 2 * FFN_HALO, d), BF16), u_stage, u_stage],
        compiler_params=_params("parallel", "parallel", "arbitrary"),
        name="conv_ffn",
    )(x, x, x, norm_g.reshape(1, d), w_up, w_up, dw, dw, dwb, dwb, w_down, final_g.reshape(1, d))


def _cast_cols_kernel(w_ref, s_ref, o_ref):
    o_ref[...] = (w_ref[...] * s_ref[...]).astype(o_ref.dtype)


def _permute_w_in(w_in):
    depth, d, n_in = w_in.shape
    n_rest = Z_COLS - N_BRANCH * D_MODEL
    scale = np.ones((n_in,), np.float32)
    scale[n_rest:] = 0.5
    for off in (Z_SW_Q, Z_NA_Q):
        scale[off - Z_GM_U:off - Z_GM_U + BR_W] = HEAD_DIM ** -0.5
    wblk = MXU_DIM
    nblk, shift = n_in // wblk, (N_BRANCH * D_MODEL) // wblk
    return pl.pallas_call(
        _cast_cols_kernel,
        grid=(depth, nblk),
        in_specs=[pl.BlockSpec((1, d, wblk), lambda l, c: (l, 0, c)),
                  pl.BlockSpec((1, wblk), lambda l, c: (0, c))],
        out_specs=pl.BlockSpec((1, d, wblk), lambda l, c: (l, 0, (c + shift) % nblk)),
        out_shape=jax.ShapeDtypeStruct((depth, d, n_in), BF16),
        compiler_params=_params("parallel", "parallel"),
        name="cast_w_in",
    )(w_in, jnp.asarray(scale).reshape(1, n_in))


def _trunk(x, p, *, tm=512):
    batch, seq, d = x.shape
    n = batch * seq
    x2 = x.reshape(n, d)
    for l in range(DEPTH):
        z = _norm_proj(x2, p["norm1_g"][l], p["w_in"][l], tm=2 * tm, tn=1792)
        o_a = _gmlp(z, p["gm_ln_g"][l], p["gm_ln_b"][l], p["gm_ws"][l], p["gm_bs"][l], tm=tm)
        o_b = _swa(z, p["sw_sink"][l], batch=batch, seq=seq, tq=tm)
        o_c = _convmod(z, p["cv_dw"][l], p["cv_dwb"][l], p["cv_ln_g"][l], p["cv_ln_b"][l],
                       batch=batch, seq=seq, tm=tm)
        o_d = _na(z, p["na_table"][l], batch=batch, seq=seq, rb=16)
        x2 = _merge_proj(z, (o_a, o_b, o_c, o_d), p["gate_b"][l], p["w_branch"][l], p["w_out"][l], x2, tm=tm)
        x2 = _ffn(x2, p["norm2_g"][l], p["w_up"][l], p["ffn_dw"][l], p["ffn_dwb"][l], p["w_down"][l],
                  p["final_g"], batch=batch, seq=seq, tm=2 * tm, hb=512, final_norm=(l == DEPTH - 1),
                  single_buffer_x=True)
    return x2.reshape(batch, seq, d)


def kernel(x_prompt, x_sample, norm1_g, w_in, gate_b, gm_ln_g, gm_ln_b, gm_ws, gm_bs, sw_sink, cv_dw, cv_dwb,
           cv_ln_g, cv_ln_b, na_rpb, w_branch, w_out, norm2_g, w_up, ffn_dw, ffn_dwb, w_down, final_g):
    half_a = np.where(np.arange(2 * FFN_HIDDEN) < FFN_HIDDEN, 0.5, 1.0).astype(np.float32)
    p = dict(norm1_g=norm1_g, w_in=_permute_w_in(w_in), gate_b=0.5 * gate_b, gm_ln_g=gm_ln_g, gm_ln_b=gm_ln_b,
             gm_ws=gm_ws, gm_bs=gm_bs, sw_sink=sw_sink, cv_dw=cv_dw, cv_dwb=cv_dwb, cv_ln_g=cv_ln_g,
             cv_ln_b=cv_ln_b, na_table=[_na_bias_table(na_rpb[l]) for l in range(DEPTH)],
             w_branch=(0.5 * w_branch).astype(BF16), w_out=w_out.astype(BF16),
             norm2_g=norm2_g, w_up=w_up.astype(BF16), ffn_dw=ffn_dw * half_a, ffn_dwb=ffn_dwb * half_a,
             w_down=w_down.astype(BF16), final_g=final_g)
    return (_trunk(x_prompt, p), _trunk(x_sample, p))
```

```python
import functools

import numpy as np
import jax
import jax.numpy as jnp
from jax import lax
from jax.experimental import pallas as pl
from jax.experimental.pallas import tpu as pltpu

F32 = jnp.float32
BF16 = jnp.bfloat16

D_MODEL = 2048
DEPTH = 2
GRID_W = 64
N_BRANCH = 4
BR_W = 512
HEAD_DIM = 64
CHUNK = 128
GM_GROUPS = 4
SW_HEADS = 8
SW_KV = 2
SW_WIN = 128
ROPE_THETA = 10000.0
CONV_W = 31
NA_HEADS = 8
NA_ROWS = 8
NA_COLS = 16
NA_QROWS = 4
NA_KROWS = 12
NA_T_LOW = 2 * NA_ROWS - 2
NA_T_HIGH = NA_T_LOW + NA_ROWS
NA_T_NONE = NA_T_HIGH + NA_ROWS
FFN_HIDDEN = 5632
EPS = 1e-6
NEG_INF = -1e30

LANES = 128
SUBLANES = 8
MXU_DIM = 256
CONV_HALO = 16
FFN_HALO = SUBLANES
FFN_ROW_CHUNK = 256

Z_GATES = 0
Z_GM_U = 8192
Z_GM_V = 8704
Z_SW_Q = 9216
Z_SW_K = 9728
Z_SW_V = 9856
Z_CV_A = 9984
Z_CV_G = 10496
Z_NA_Q = 11008
Z_NA_K = 11520
Z_NA_V = 12032
Z_COLS = 12544
HALF_W = BR_W // 2

VMEM_LIMIT = 56 * 1024 * 1024


def _params(*sem):
    return pltpu.CompilerParams(dimension_semantics=sem, vmem_limit_bytes=VMEM_LIMIT)


def _norm_proj_kernel(x_ref, g_ref, w_ref, o_ref, h_ref):
    @pl.when(pl.program_id(1) == 0)
    def _():
        x = x_ref[...]
        ms = jnp.mean(x * x, axis=-1, keepdims=True)
        h_ref[...] = (x * lax.rsqrt(ms + EPS) * g_ref[...]).astype(BF16)

    o_ref[...] = jnp.dot(h_ref[...], w_ref[...], preferred_element_type=F32).astype(o_ref.dtype)


def _norm_proj(x, g, w, layer, *, tm, tn):
    n, d = x.shape
    nout = w.shape[2]
    return pl.pallas_call(
        _norm_proj_kernel,
        grid=(n // tm, nout // tn),
        in_specs=[
            pl.BlockSpec((tm, d), lambda i, j: (i, 0)),
            pl.BlockSpec((1, d), lambda i, j: (0, 0)),
            pl.BlockSpec((None, d, tn), lambda i, j: (layer, 0, j)),
        ],
        out_specs=pl.BlockSpec((tm, tn), lambda i, j: (i, j)),
        out_shape=jax.ShapeDtypeStruct((n, nout), BF16),
        scratch_shapes=[pltpu.VMEM((tm, d), BF16)],
        compiler_params=_params("parallel", "arbitrary"),
        name="norm_proj",
    )(x, g.reshape(1, d), w)


def _layernorm_rows(x, g, b):
    mu = jnp.mean(x, axis=-1, keepdims=True)
    xc = x - mu
    var = jnp.mean(xc * xc, axis=-1, keepdims=True)
    return xc * lax.rsqrt(var + EPS) * g + b


def _gmlp_kernel(u_ref, v_ref, lng_ref, lnb_ref, ws_ref, bs_ref, o_ref, *, tm):
    vn = _layernorm_rows(v_ref[...].astype(F32), lng_ref[...], lnb_ref[...]).astype(BF16)
    gw = BR_W // GM_GROUPS
    for c in range(tm // CHUNK):
        rows = slice(c * CHUNK, (c + 1) * CHUNK)
        for g in range(GM_GROUPS):
            cols = slice(g * gw, (g + 1) * gw)
            mixed = jnp.dot(ws_ref[g], vn[rows, cols], preferred_element_type=F32) + bs_ref[:, cols]
            o_ref[rows, cols] = (u_ref[rows, cols].astype(F32) * mixed).astype(o_ref.dtype)


def _gmlp(z, ln_g, ln_b, ws, bs, *, tm):
    n = z.shape[0]
    gw = BR_W // GM_GROUPS
    bs_full = jnp.repeat(bs.T, gw, axis=1)
    const = lambda i: (0, 0)
    return pl.pallas_call(
        functools.partial(_gmlp_kernel, tm=tm),
        grid=(n // tm,),
        in_specs=[
            pl.BlockSpec((tm, BR_W), lambda i: (i, Z_GM_U // BR_W)),
            pl.BlockSpec((tm, BR_W), lambda i: (i, Z_GM_V // BR_W)),
            pl.BlockSpec((1, BR_W), const),
            pl.BlockSpec((1, BR_W), const),
            pl.BlockSpec((GM_GROUPS, CHUNK, CHUNK), lambda i: (0, 0, 0)),
            pl.BlockSpec((CHUNK, BR_W), const),
        ],
        out_specs=pl.BlockSpec((tm, BR_W), lambda i: (i, 0)),
        out_shape=jax.ShapeDtypeStruct((n, BR_W), BF16),
        compiler_params=_params("parallel"),
        name="gmlp",
    )(z, z, ln_g.reshape(1, BR_W), ln_b.reshape(1, BR_W), ws.astype(BF16), bs_full)


def _rope(x, cos, sin, first_half):
    w = x.shape[1]
    reps = w // LANES
    if reps > 1:
        cos = jnp.concatenate([cos] * reps, axis=1)
        sin = jnp.concatenate([sin] * reps, axis=1)
        first_half = jnp.concatenate([first_half] * reps, axis=1)
    half = HEAD_DIM // 2
    lower = pltpu.roll(x, half, 1)
    upper = pltpu.roll(x, w - half, 1)
    return x * cos + jnp.where(first_half, -upper, lower) * sin


def _swa_kernel(sink_ref, q_ref, kp_ref, k_ref, kn_ref, vp_ref, v_ref, vn_ref, cos_ref, sin_ref,
                o_ref, qs_ref, kd_ref, vd_ref, *, tq, seq):
    t = pl.program_id(1)
    base = t * tq
    blk = SW_WIN
    lane = lax.broadcasted_iota(jnp.int32, (1, LANES), 1)
    first_half = (lane % HEAD_DIM) < (HEAD_DIM // 2)
    lane_lo = lane < HEAD_DIM

    def table(ref, start, size):
        return ref[pl.ds(pl.multiple_of(start, blk), size), :]

    p_start = jnp.maximum(base - blk, 0)
    n_start = jnp.minimum(base + tq, seq - blk)
    cq, sq = table(cos_ref, base, tq), table(sin_ref, base, tq)
    qs_ref[...] = _rope(q_ref[...].astype(F32), cq, sq, first_half).astype(BF16)
    k_ext = jnp.concatenate([
        _rope(kp_ref[...].astype(F32), table(cos_ref, p_start, blk), table(sin_ref, p_start, blk), first_half),
        _rope(k_ref[...].astype(F32), cq, sq, first_half),
        _rope(kn_ref[...].astype(F32), table(cos_ref, n_start, blk), table(sin_ref, n_start, blk), first_half),
    ], axis=0)
    v_ext = jnp.concatenate([vp_ref[...], v_ref[...], vn_ref[...]], axis=0).astype(F32)
    for src, dst in ((k_ext, kd_ref), (v_ext, vd_ref)):
        swapped = pltpu.roll(src, HEAD_DIM, 1)
        dst[0] = jnp.where(lane_lo, src, swapped).astype(BF16)
        dst[1] = jnp.where(lane_lo, swapped, src).astype(BF16)

    g = SW_HEADS // SW_KV
    qi_idx = lax.broadcasted_iota(jnp.int32, (blk, 3 * blk), 0)
    ki_idx = lax.broadcasted_iota(jnp.int32, (blk, 3 * blk), 1)
    band = (ki_idx >= qi_idx) & (ki_idx - qi_idx <= 2 * SW_WIN)
    ones_cols = jnp.ones((3 * blk, LANES), BF16)
    for i in range(tq // blk):
        kpos = base + (i - 1) * blk + ki_idx
        valid = band & (kpos >= 0) & (kpos < seq)
        qi = qs_ref[i * blk:(i + 1) * blk, :]
        outs = []
        for j in range(SW_KV):
            kj = kd_ref[j, i * blk:(i + 3) * blk, :]
            vj = vd_ref[j, i * blk:(i + 3) * blk, :]
            parts = []
            for gi in range(g):
                h = g * j + gi
                qp = qi[:, (h // 2) * LANES:(h // 2 + 1) * LANES]
                keep = lane_lo if h % 2 == 0 else jnp.logical_not(lane_lo)
                parts.append(jnp.where(keep, qp, jnp.zeros_like(qp)))
            s_all = lax.dot_general(jnp.concatenate(parts, axis=0), kj, (((1,), (1,)), ((), ())),
                                    preferred_element_type=F32)
            ps, sink_p = [], []
            for gi in range(g):
                sink = sink_ref[g * j + gi]
                s = jnp.where(valid, s_all[gi * blk:(gi + 1) * blk], NEG_INF)
                m = jnp.maximum(jnp.max(s, axis=-1, keepdims=True), sink)
                ps.append(jnp.exp(s - m).astype(BF16))
                sink_p.append(jnp.exp(sink - m))
            o = jnp.dot(jnp.concatenate(ps, axis=0), jnp.concatenate([vj, ones_cols], axis=1),
                        preferred_element_type=F32)
            outs.append(o[:, :LANES] / (o[:, LANES:] + jnp.concatenate(sink_p, axis=0)))
        for pr in range(SW_HEADS // 2):
            j, g0 = (2 * pr) // g, (2 * pr) % g
            pair = jnp.where(lane_lo, outs[j][g0 * blk:(g0 + 1) * blk], outs[j][(g0 + 1) * blk:(g0 + 2) * blk])
            o_ref[i * blk:(i + 1) * blk, pr * LANES:(pr + 1) * LANES] = pair.astype(o_ref.dtype)


def _rope_tables(seq):
    half = HEAD_DIM // 2
    inv = jnp.power(ROPE_THETA, -jnp.arange(half, dtype=F32) / half)
    ang = jnp.arange(seq, dtype=F32)[:, None] * inv[None, :]
    reps = LANES // half
    return jnp.tile(jnp.cos(ang), (1, reps)), jnp.tile(jnp.sin(ang), (1, reps))


def _swa(z, sink, *, batch, seq, tq):
    n = z.shape[0]
    blk = SW_WIN
    nt = seq // tq
    r = tq // blk
    cos_t, sin_t = _rope_tables(seq)
    kw = SW_KV * HEAD_DIM
    kcol, vcol = Z_SW_K // kw, Z_SW_V // kw
    nblk = n // blk

    def main(col):
        return lambda b, t: (b * nt + t, col)

    def prev(col):
        return lambda b, t: (jnp.maximum((b * nt + t) * r - 1, 0), col)

    def nxt(col):
        return lambda b, t: (jnp.minimum((b * nt + t + 1) * r, nblk - 1), col)

    const = lambda b, t: (0, 0)
    return pl.pallas_call(
        functools.partial(_swa_kernel, tq=tq, seq=seq),
        grid=(batch, nt),
        in_specs=[
            pl.BlockSpec(memory_space=pltpu.SMEM),
            pl.BlockSpec((tq, BR_W), main(Z_SW_Q // BR_W)),
            pl.BlockSpec((blk, kw), prev(kcol)),
            pl.BlockSpec((tq, kw), main(kcol)),
            pl.BlockSpec((blk, kw), nxt(kcol)),
            pl.BlockSpec((blk, kw), prev(vcol)),
            pl.BlockSpec((tq, kw), main(vcol)),
            pl.BlockSpec((blk, kw), nxt(vcol)),
            pl.BlockSpec((seq, LANES), const),
            pl.BlockSpec((seq, LANES), const),
        ],
        out_specs=pl.BlockSpec((tq, BR_W), lambda b, t: (b * nt + t, 0)),
        out_shape=jax.ShapeDtypeStruct((n, BR_W), BF16),
        scratch_shapes=[
            pltpu.VMEM((tq, BR_W), BF16),
            pltpu.VMEM((SW_KV, tq + 2 * blk, LANES), BF16),
            pltpu.VMEM((SW_KV, tq + 2 * blk, LANES), BF16),
        ],
        compiler_params=_params("parallel", "parallel"),
        name="swa",
    )(sink, z, z, z, z, z, z, z, cos_t, sin_t)


def _convmod_kernel(a0_ref, a1_ref, g0_ref, g1_ref, ap0_ref, ap1_ref, gp0_ref, gp1_ref, an0_ref, an1_ref,
                    gn0_ref, gn1_ref, dw_ref, dwb_ref, lng_ref, lnb_ref, o_ref, xs_ref, *, tm, nt, rc):
    t = pl.program_id(1)
    hal = CONV_HALO
    rows = tm + 2 * hal

    def glu(a0, a1, g0, g1):
        a = jnp.concatenate([a0[...], a1[...]], axis=1).astype(F32)
        return a * jax.nn.sigmoid(jnp.concatenate([g0[...], g1[...]], axis=1).astype(F32))

    xs_ref[0, 0:hal, :] = glu(ap0_ref, ap1_ref, gp0_ref, gp1_ref) * (t > 0).astype(F32)
    xs_ref[0, hal:hal + tm, :] = glu(a0_ref, a1_ref, g0_ref, g1_ref)
    xs_ref[0, hal + tm:rows, :] = glu(an0_ref, an1_ref, gn0_ref, gn1_ref) * (t < nt - 1).astype(F32)
    for r0 in range(0, rows - SUBLANES, rc):
        nr = min(rc, rows - SUBLANES - r0)
        x = xs_ref[0, r0:r0 + nr + SUBLANES, :]
        for s in range(1, SUBLANES):
            xs_ref[s, r0:r0 + nr, :] = pltpu.roll(x, nr + SUBLANES - s, 0)[0:nr]
    pad = CONV_W // 2
    for c in range(tm // rc):
        r0 = c * rc
        acc = jnp.zeros((rc, BR_W), F32)
        for k in range(CONV_W):
            off = hal - pad + k
            al = r0 + (off // SUBLANES) * SUBLANES
            acc = acc + dw_ref[k:k + 1, :] * xs_ref[off % SUBLANES, al:al + rc, :]
        y = _layernorm_rows(acc + dwb_ref[...], lng_ref[...], lnb_ref[...])
        o_ref[r0:r0 + rc, :] = (y * jax.nn.sigmoid(y)).astype(o_ref.dtype)


def _halo_specs(width, col, *, nt, tm, halo, nrows):
    r = tm // halo
    last = nrows // halo - 1
    return (
        pl.BlockSpec((tm, width), lambda b, t: (b * nt + t, col)),
        pl.BlockSpec((halo, width), lambda b, t: (jnp.maximum((b * nt + t) * r - 1, 0), col)),
        pl.BlockSpec((halo, width), lambda b, t: (jnp.minimum((b * nt + t + 1) * r, last), col)),
    )


def _convmod(z, dw, dwb, ln_g, ln_b, *, batch, seq, tm):
    n = z.shape[0]
    nt = seq // tm
    groups = [_halo_specs(HALF_W, off // HALF_W + i, nt=nt, tm=tm, halo=CONV_HALO, nrows=n)
              for off in (Z_CV_A, Z_CV_G) for i in range(2)]
    specs = [grp[pos] for pos in range(3) for grp in groups]
    const = lambda b, t: (0, 0)
    row = lambda v: v.reshape(1, BR_W)
    return pl.pallas_call(
        functools.partial(_convmod_kernel, tm=tm, nt=nt, rc=64),
        grid=(batch, nt),
        in_specs=specs + [
            pl.BlockSpec((CONV_W, BR_W), const),
            pl.BlockSpec((1, BR_W), const),
            pl.BlockSpec((1, BR_W), const),
            pl.BlockSpec((1, BR_W), const)],
        out_specs=pl.BlockSpec((tm, BR_W), lambda b, t: (b * nt + t, 0)),
        out_shape=jax.ShapeDtypeStruct((n, BR_W), BF16),
        scratch_shapes=[pltpu.VMEM((SUBLANES, tm + 2 * CONV_HALO, BR_W), F32)],
        compiler_params=_params("parallel", "parallel"),
        name="convmod",
    )(*([z] * len(specs)), dw, row(dwb), row(ln_g), row(ln_b))


def _na_kernel(q0_ref, q1_ref, k0_ref, k1_ref, v0_ref, v1_ref, bias_ref, o_ref, *, rb, rows):
    lane = lax.broadcasted_iota(jnp.int32, (1, LANES), 1)
    lane_lo = lane < HEAD_DIM
    row0 = pl.program_id(1) * rb
    npair = NA_KROWS // 2
    ones_cols = jnp.ones((NA_KROWS * GRID_W, LANES), BF16)

    def one_group(gi, carry):
        r0 = row0 + gi * NA_QROWS
        ks = jnp.clip(r0 - NA_ROWS // 2, 0, rows - NA_KROWS)
        q_rows = pl.ds(pl.multiple_of(gi * (NA_QROWS * GRID_W), NA_QROWS * GRID_W), NA_QROWS * GRID_W)
        k_rows = pl.ds(pl.multiple_of(ks * GRID_W, GRID_W), NA_KROWS * GRID_W)
        entry = []
        for qi in range(NA_QROWS):
            r = r0 + qi
            rs = jnp.clip(r - NA_ROWS // 2, 0, rows - NA_ROWS)
            for w in range(npair):
                kr = ks + 2 * w
                d = kr - r + NA_ROWS - 1
                in0 = (kr >= rs) & (kr < rs + NA_ROWS)
                in1 = (kr + 1 >= rs) & (kr + 1 < rs + NA_ROWS)
                entry.append(jnp.where(in0 & in1, d,
                                       jnp.where(in1, NA_T_LOW + d + 1,
                                                 jnp.where(in0, NA_T_HIGH + d - (NA_ROWS - 1), NA_T_NONE))))
        for pr in range(NA_HEADS // 2):
            cols = slice(pr * LANES, (pr + 1) * LANES)
            hcols = slice((pr % 2) * LANES, (pr % 2 + 1) * LANES)
            qp = (q0_ref, q1_ref)[pr // 2][q_rows, hcols]
            kp = (k0_ref, k1_ref)[pr // 2][k_rows, hcols]
            vp = (v0_ref, v1_ref)[pr // 2][k_rows, hcols]
            halves = []
            for hh in range(2):
                h = 2 * pr + hh
                keep = lane_lo if hh == 0 else jnp.logical_not(lane_lo)
                qm = jnp.where(keep, qp, jnp.zeros_like(qp))
                s = lax.dot_general(qm, kp, (((1,), (1,)), ((), ())), preferred_element_type=F32)
                bias = jnp.concatenate([
                    jnp.concatenate([bias_ref[h, entry[qi * npair + w]] for w in range(npair)], axis=1)
                    for qi in range(NA_QROWS)], axis=0)
                s = s + bias
                m = jnp.max(s, axis=-1, keepdims=True)
                p = jnp.exp(s - m).astype(BF16)
                pv = jnp.dot(p, jnp.concatenate([vp, ones_cols], axis=1), preferred_element_type=F32)
                halves.append(pv[:, :LANES] / pv[:, LANES:])
            o_ref[q_rows, cols] = jnp.where(lane_lo, halves[0], halves[1]).astype(o_ref.dtype)
        return carry

    lax.fori_loop(0, rb // NA_QROWS, one_group, 0)


def _na_bias_table(rpb):
    c = np.arange(GRID_W)
    cs = np.clip(c - NA_COLS // 2, 0, GRID_W - NA_COLS)
    kc = np.arange(GRID_W)
    ok = (kc[None, :] >= cs[:, None]) & (kc[None, :] < cs[:, None] + NA_COLS)
    dc = np.clip(kc[None, :] - c[:, None], 1 - NA_COLS, NA_COLS - 1) + NA_COLS - 1
    full = jnp.where(ok[None, None], rpb.astype(F32)[:, :, dc], NEG_INF)
    out = jnp.full_like(full[:, :NA_ROWS], NEG_INF)
    return jnp.concatenate([
        jnp.concatenate([full[:, :-1], full[:, 1:]], axis=-1),
        jnp.concatenate([out, full[:, :NA_ROWS]], axis=-1),
        jnp.concatenate([full[:, NA_ROWS - 1:], out], axis=-1),
        jnp.concatenate([out[:, :1], out[:, :1]], axis=-1)], axis=1)


def _na(z, table, *, batch, seq, rb):
    n = z.shape[0]
    rows = seq // GRID_W
    nrb = rows // rb
    return pl.pallas_call(
        functools.partial(_na_kernel, rb=rb, rows=rows),
        grid=(batch, nrb),
        in_specs=[
            pl.BlockSpec((rb * GRID_W, HALF_W), lambda b, t: (b * nrb + t, Z_NA_Q // HALF_W)),
            pl.BlockSpec((rb * GRID_W, HALF_W), lambda b, t: (b * nrb + t, Z_NA_Q // HALF_W + 1)),
            pl.BlockSpec((seq, HALF_W), lambda b, t: (b, Z_NA_K // HALF_W)),
            pl.BlockSpec((seq, HALF_W), lambda b, t: (b, Z_NA_K // HALF_W + 1)),
            pl.BlockSpec((seq, HALF_W), lambda b, t: (b, Z_NA_V // HALF_W)),
            pl.BlockSpec((seq, HALF_W), lambda b, t: (b, Z_NA_V // HALF_W + 1)),
            pl.BlockSpec(table.shape, lambda b, t: (0, 0, 0, 0), pipeline_mode=pl.Buffered(1)),
        ],
        out_specs=pl.BlockSpec((rb * GRID_W, BR_W), lambda b, t: (b * nrb + t, 0)),
        out_shape=jax.ShapeDtypeStruct((n, BR_W), BF16),
        compiler_params=_params("parallel", "arbitrary"),
        name="nattn",
    )(z, z, z, z, z, z, table)


def _merge_proj_kernel(oa_ref, ob_ref, oc_ref, od_ref, g_ref, bias_ref, wb_ref, wo_ref, x_ref, o_ref):
    s = pl.program_id(1)
    tm, d = o_ref.shape
    branches = (oa_ref, ob_ref, oc_ref, od_ref)
    per_step = N_BRANCH // 2
    for ss in range(2):
        @pl.when(s == ss)
        def _(ss=ss):
            for r0 in range(0, tm, MXU_DIM):
                rows = slice(r0, r0 + MXU_DIM)
                acc = None
                for kk in range(per_step * ss, per_step * (ss + 1)):
                    gcols = slice((kk % per_step) * d, (kk % per_step + 1) * d)
                    y = jnp.dot(branches[kk][rows, :], wb_ref[kk], preferred_element_type=F32)
                    th = jnp.tanh(g_ref[rows, gcols].astype(F32) + bias_ref[kk:kk + 1, :])
                    term = y * (th + 1.0)
                    acc = term if acc is None else acc + term
                if ss == 0:
                    o_ref[rows, :] = acc
                else:
                    merged = (o_ref[rows, :] + acc).astype(BF16)
                    o_ref[rows, :] = x_ref[rows, :] + jnp.dot(merged, wo_ref[...], preferred_element_type=F32)


def _merge_proj(z, branches, gate_b, w_branch, w_out, x, layer, *, tm):
    n, d = x.shape
    per_step = N_BRANCH // 2
    br_spec = pl.BlockSpec((tm, BR_W), lambda i, s: (i, 0))
    return pl.pallas_call(
        _merge_proj_kernel,
        grid=(n // tm, 2),
        in_specs=[br_spec] * N_BRANCH + [
            pl.BlockSpec((tm, per_step * d), lambda i, s: (i, Z_GATES // (per_step * d) + s)),
            pl.BlockSpec((N_BRANCH, d), lambda i, s: (0, 0)),
            pl.BlockSpec((None, N_BRANCH, BR_W, d), lambda i, s: (layer, 0, 0, 0), pipeline_mode=pl.Buffered(1)),
            pl.BlockSpec((None, d, d), lambda i, s: (layer, 0, 0), pipeline_mode=pl.Buffered(1)),
            pl.BlockSpec((tm, d), lambda i, s: (i, 0)),
        ],
        out_specs=pl.BlockSpec((tm, d), lambda i, s: (i, 0)),
        out_shape=jax.ShapeDtypeStruct((n, d), F32),
        compiler_params=_params("parallel", "arbitrary"),
        name="merge_proj",
    )(*branches, z, gate_b, w_branch, w_out, x)


def _ffn_kernel(x_ref, xp_ref, xn_ref, g_ref, wa_ref, wb_ref, dwa_ref, dwb_ref, ba_ref, bb_ref, wd_ref, fg_ref,
                o_ref, h_ref, u0_ref, u1_ref, *, tm, nt, nh, hb, final_norm):
    t = pl.program_id(1)
    j = pl.program_id(2)
    rows = tm + 2 * FFN_HALO
    u_stage = (u0_ref, u1_ref)

    def norm(x):
        ms = jnp.mean(x * x, axis=-1, keepdims=True)
        return x * lax.rsqrt(ms + EPS) * g_ref[...]

    def up(dst_ref):
        h = h_ref[...]
        dst_ref[:, 0:hb] = jnp.dot(h, wa_ref[...], preferred_element_type=F32)
        dst_ref[:, hb:2 * hb] = jnp.dot(h, wb_ref[...], preferred_element_type=F32)

    def down(src_ref):
        part = None
        rc = FFN_ROW_CHUNK
        for c in range(hb // MXU_DIM):
            ca = slice(c * MXU_DIM, (c + 1) * MXU_DIM)
            cb = slice(hb + c * MXU_DIM, hb + (c + 1) * MXU_DIM)
            pair = lambda ra, rb, rows_: jnp.concatenate([ra[rows_, ca], rb[rows_, ca]], axis=1)
            taps = [pair(dwa_ref, dwb_ref, slice(k, k + 1)) for k in range(3)]
            bias = pair(ba_ref, bb_ref, slice(0, 1))
            gated = []
            for r0 in range(0, tm, rc):
                lo = slice(rows - FFN_HALO, rows) if r0 == 0 else slice(r0 - FFN_HALO, r0)
                body = slice(r0, r0 + rc + FFN_HALO)
                u = jnp.concatenate([
                    jnp.concatenate([src_ref[lo, ca], src_ref[body, ca]], axis=0),
                    jnp.concatenate([src_ref[lo, cb], src_ref[body, cb]], axis=0)], axis=1)
                n_u = rc + 2 * FFN_HALO
                keep = slice(FFN_HALO, FFN_HALO + rc)
                ab = (taps[0] * pltpu.roll(u, 1, 0)[keep] + taps[1] * u[keep]
                      + taps[2] * pltpu.roll(u, n_u - 1, 0)[keep] + bias)
                a, b = ab[:, :MXU_DIM], ab[:, MXU_DIM:]
                gated.append(((a * b) * (jnp.tanh(a) + 1.0)).astype(BF16))
            pc = jnp.dot(jnp.concatenate(gated, axis=0), wd_ref[c * MXU_DIM:(c + 1) * MXU_DIM, :],
                         preferred_element_type=F32)
            part = pc if part is None else part + pc
        o_ref[...] += part

    @pl.when(j == 0)
    def _():
        x = x_ref[...]
        h_ref[0:tm, :] = norm(x).astype(BF16)
        h_ref[tm:rows, :] = jnp.concatenate([
            norm(xn_ref[...]) * (t < nt - 1).astype(F32),
            norm(xp_ref[...]) * (t > 0).astype(F32)], axis=0).astype(BF16)
        o_ref[...] = x
        up(u_stage[0])

    for p in (0, 1):
        @pl.when((j > 0) & (j < nh) & (j % 2 == p))
        def _(p=p):
            up(u_stage[p])
            down(u_stage[1 - p])

    @pl.when(j == nh)
    def _():
        down(u_stage[(nh - 1) % 2])
        if final_norm:
            y = o_ref[...]
            ms = jnp.mean(y * y, axis=-1, keepdims=True)
            o_ref[...] = y * lax.rsqrt(ms + EPS) * fg_ref[...]


def _ffn(x, norm_g, w_up, dw, dwb, w_down, final_g, layer, *, batch, seq, tm, hb, final_norm, single_buffer_x):
    n, d = x.shape
    nt = seq // tm
    nh = FFN_HIDDEN // hb
    r = tm // FFN_HALO
    last = n // FFN_HALO - 1
    const = lambda b, t, j: (0, 0)
    up_a = lambda b, t, j: (layer, 0, jnp.minimum(j, nh - 1))
    up_b = lambda b, t, j: (layer, 0, nh + jnp.minimum(j, nh - 1))
    gate_a = lambda b, t, j: (0, jnp.maximum(j - 1, 0))
    gate_b = lambda b, t, j: (0, nh + jnp.maximum(j - 1, 0))
    u_stage = pltpu.VMEM((tm + 2 * FFN_HALO, 2 * hb), F32)
    dwb = dwb.reshape(1, -1)
    x_mode = dict(pipeline_mode=pl.Buffered(1)) if single_buffer_x else {}
    return pl.pallas_call(
        functools.partial(_ffn_kernel, tm=tm, nt=nt, nh=nh, hb=hb, final_norm=final_norm),
        grid=(batch, nt, nh + 1),
        in_specs=[
            pl.BlockSpec((tm, d), lambda b, t, j: (b * nt + t, 0), **x_mode),
            pl.BlockSpec((FFN_HALO, d), lambda b, t, j: (jnp.maximum((b * nt + t) * r - 1, 0), 0)),
            pl.BlockSpec((FFN_HALO, d), lambda b, t, j: (jnp.minimum((b * nt + t + 1) * r, last), 0)),
            pl.BlockSpec((1, d), const),
            pl.BlockSpec((None, d, hb), up_a),
            pl.BlockSpec((None, d, hb), up_b),
            pl.BlockSpec((3, hb), gate_a),
            pl.BlockSpec((3, hb), gate_b),
            pl.BlockSpec((1, hb), gate_a),
            pl.BlockSpec((1, hb), gate_b),
            pl.BlockSpec((None, hb, d), lambda b, t, j: (layer, jnp.maximum(j - 1, 0), 0)),
            pl.BlockSpec((1, d), const),
        ],
        out_specs=pl.BlockSpec((tm, d), lambda b, t, j: (b * nt + t, 0)),
        out_shape=jax.ShapeDtypeStruct((n, d), F32),
        scratch_shapes=[pltpu.VMEM((tm + 2 * FFN_HALO, d), BF16), u_stage, u_stage],
        compiler_params=_params("parallel", "parallel", "arbitrary"),
        name="conv_ffn",
    )(x, x, x, norm_g.reshape(1, d), w_up, w_up, dw, dw, dwb, dwb, w_down, final_g.reshape(1, d))


def _cast_cols_kernel(w_ref, s_ref, o_ref):
    o_ref[...] = (w_ref[...] * s_ref[...]).astype(o_ref.dtype)


def _permute_w_in(w_in):
    depth, d, n_in = w_in.shape
    n_rest = Z_COLS - N_BRANCH * D_MODEL
    scale = np.ones((n_in,), np.float32)
    scale[n_rest:] = 0.5
    for off in (Z_SW_Q, Z_NA_Q):
        scale[off - Z_GM_U:off - Z_GM_U + BR_W] = HEAD_DIM ** -0.5
    wblk = MXU_DIM
    nblk, shift = n_in // wblk, (N_BRANCH * D_MODEL) // wblk
    return pl.pallas_call(
        _cast_cols_kernel,
        grid=(depth, nblk),
        in_specs=[pl.BlockSpec((1, d, wblk), lambda l, c: (l, 0, c)),
                  pl.BlockSpec((1, wblk), lambda l, c: (0, c))],
        out_specs=pl.BlockSpec((1, d, wblk), lambda l, c: (l, 0, (c + shift) % nblk)),
        out_shape=jax.ShapeDtypeStruct((depth, d, n_in), BF16),
        compiler_params=_params("parallel", "parallel"),
        name="cast_w_in",
    )(w_in, jnp.asarray(scale).reshape(1, n_in))


def _trunk(x, p, *, tm=512):
    batch, seq, d = x.shape
    n = batch * seq
    x2 = x.reshape(n, d)
    for l in range(DEPTH):
        z = _norm_proj(x2, p["norm1_g"][l], p["w_in"], l, tm=2 * tm, tn=1792)
        o_a = _gmlp(z, p["gm_ln_g"][l], p["gm_ln_b"][l], p["gm_ws"][l], p["gm_bs"][l], tm=tm)
        o_b = _swa(z, p["sw_sink"][l], batch=batch, seq=seq, tq=tm)
        o_c = _convmod(z, p["cv_dw"][l], p["cv_dwb"][l], p["cv_ln_g"][l], p["cv_ln_b"][l],
                       batch=batch, seq=seq, tm=tm)
        o_d = _na(z, p["na_table"][l], batch=batch, seq=seq, rb=16)
        x2 = _merge_proj(z, (o_a, o_b, o_c, o_d), p["gate_b"][l], p["w_branch"], p["w_out"], x2, l, tm=tm)
        x2 = _ffn(x2, p["norm2_g"][l], p["w_up"], p["ffn_dw"][l], p["ffn_dwb"][l], p["w_down"],
                  p["final_g"], l, batch=batch, seq=seq, tm=2 * tm, hb=512, final_norm=(l == DEPTH - 1),
                  single_buffer_x=True)
    return x2.reshape(batch, seq, d)


def kernel(x_prompt, x_sample, norm1_g, w_in, gate_b, gm_ln_g, gm_ln_b, gm_ws, gm_bs, sw_sink, cv_dw, cv_dwb,
           cv_ln_g, cv_ln_b, na_rpb, w_branch, w_out, norm2_g, w_up, ffn_dw, ffn_dwb, w_down, final_g):
    half_a = np.where(np.arange(2 * FFN_HIDDEN) < FFN_HIDDEN, 0.5, 1.0).astype(np.float32)
    p = dict(norm1_g=norm1_g, w_in=_permute_w_in(w_in), gate_b=0.5 * gate_b, gm_ln_g=gm_ln_g, gm_ln_b=gm_ln_b,
             gm_ws=gm_ws, gm_bs=gm_bs, sw_sink=sw_sink, cv_dw=cv_dw, cv_dwb=cv_dwb, cv_ln_g=cv_ln_g,
             cv_ln_b=cv_ln_b, na_table=[_na_bias_table(na_rpb[l]) for l in range(DEPTH)],
             w_branch=(0.5 * w_branch).astype(BF16), w_out=w_out.astype(BF16),
             norm2_g=norm2_g, w_up=w_up.astype(BF16), ffn_dw=ffn_dw * half_a, ffn_dwb=ffn_dwb * half_a,
             w_down=w_down.astype(BF16), final_g=final_g)
    return (_trunk(x_prompt, p), _trunk(x_sample, p))
```
